```python
import math
import jax, jax.numpy as jnp
from jax import lax
import numpy as np

D_MODEL = 2048
BATCH = 4
SEQ = 4096
DEPTH = 4
DEC_BATCH = 16
DEC_SEQ = 16
PAST_LEN = 1024

CHUNK = 64
PLE_DIM = 256
POOL_W = 1024
POOL_WINDOWS = (2, 4, 8, 16)
POOL_GROUPS = len(POOL_WINDOWS)
POOL_GW = POOL_W // POOL_GROUPS
POOL_HIST = max(POOL_WINDOWS) - 1
RET_HEADS = 4
RET_DK = 256
RET_DV = 256
RET_W = RET_HEADS * RET_DV
MIX_W = POOL_W + RET_W
IN_W = 2 * POOL_W + 2 * RET_HEADS * RET_DK + 2 * RET_W
ROPE_BASE = 10000.0
EPS = 1e-6
GN_EPS = 1e-5

kernel_name = 'hymba_pool_retnet_streaming_step'


def rmsnorm(x, g):
    x32 = x.astype(jnp.float32)
    y = x32 * lax.rsqrt(jnp.mean(x32 * x32, axis=-1, keepdims=True) + EPS)
    return (y * g.astype(jnp.float32)).astype(x.dtype)


def rotary(x, pos):
    half = x.shape[-1] // 2
    inv = 1.0 / (ROPE_BASE ** (jnp.arange(half, dtype=jnp.float32) / half))
    ang = pos[:, None] * inv[None, :]
    cos = jnp.cos(ang)[None, :, None, :]
    sin = jnp.sin(ang)[None, :, None, :]
    x32 = x.astype(jnp.float32)
    x1, x2 = x32[..., :half], x32[..., half:]
    return jnp.concatenate([x1 * cos - x2 * sin, x1 * sin + x2 * cos], axis=-1)


def pool_mix(u, hist, offset, w_pool, pool_scale):
    B, L, _ = u.shape
    u32 = u.astype(jnp.float32)
    z = jnp.concatenate([hist.astype(jnp.float32), u32], axis=1)
    cs = jnp.concatenate([jnp.zeros((B, 1, POOL_W), jnp.float32),
                          jnp.cumsum(z, axis=1)], axis=1)
    pos = (offset + jnp.arange(L)).astype(jnp.float32)
    outs = []
    for g, w in enumerate(POOL_WINDOWS):
        sl = slice(g * POOL_GW, (g + 1) * POOL_GW)
        s = cs[:, POOL_HIST + 1:POOL_HIST + 1 + L, sl] - cs[:, POOL_HIST + 1 - w:POOL_HIST + 1 - w + L, sl]
        cnt = jnp.minimum(jnp.float32(w), pos + 1.0)
        outs.append(s / cnt[None, :, None])
    pooled = jnp.concatenate(outs, axis=-1) - u32
    pooled = pooled.reshape(B, L, POOL_GROUPS, POOL_GW)
    y = jnp.einsum('blgc,gcd->blgd', pooled, w_pool.astype(jnp.float32)).reshape(B, L, POOL_W)
    y = y * pool_scale.astype(jnp.float32)
    new_hist = z[:, -POOL_HIST:, :].astype(hist.dtype)
    return y, new_hist


def retention(q, k, v, S0):
    B, L, H, _ = q.shape
    C = min(CHUNK, L)
    nC = L // C
    lg = jnp.log(1.0 - 2.0 ** (-5.0 - jnp.arange(H, dtype=jnp.float32)))
    idx = jnp.arange(C, dtype=jnp.float32)
    diff = idx[:, None] - idx[None, :]
    D = jnp.where(diff[None] >= 0, jnp.exp(jnp.maximum(diff, 0.0)[None] * lg[:, None, None]), 0.0)
    q_decay = jnp.exp((idx[None, :] + 1.0) * lg[:, None])
    k_decay = jnp.exp((C - 1.0 - idx[None, :]) * lg[:, None])
    chunk_decay = jnp.exp(C * lg)

    def to_chunks(t):
        return t.reshape(B, nC, C, H, t.shape[-1]).transpose(1, 0, 3, 2, 4)

    qs, ks, vs = to_chunks(q), to_chunks(k), to_chunks(v.astype(jnp.float32))

    def step(S, inp):
        qc, kc, vc = inp
        scores = jnp.einsum('bhnd,bhmd->bhnm', qc, kc) * D[None]
        o_in = jnp.einsum('bhnm,bhmv->bhnv', scores, vc)
        o_x = jnp.einsum('bhnd,bhdv->bhnv', qc * q_decay[None, :, :, None], S)
        S_new = S * chunk_decay[None, :, None, None] + jnp.einsum(
            'bhmd,bhmv->bhdv', kc * k_decay[None, :, :, None], vc)
        return S_new, o_in + o_x

    S_fin, o = lax.scan(step, S0.astype(jnp.float32), (qs, ks, vs))
    o = o.transpose(1, 0, 3, 2, 4).reshape(B, L, H, RET_DV)
    return o, S_fin.astype(S0.dtype)


def layer(h, p_i, hist, S0, offset, g_mix, w_in, w_pool, pool_scale, g_gn, w_out, g_ple, w_pg, w_ple):
    B, L, _ = h.shape
    hn = rmsnorm(h, g_mix)
    z = hn @ w_in
    u, gp, q, k, v, gr = jnp.split(z, 6, axis=-1)
    y_pool, new_hist = pool_mix(u, hist, offset, w_pool, pool_scale)
    y_pool = y_pool * jax.nn.silu(gp.astype(jnp.float32))
    pos = (offset + jnp.arange(L)).astype(jnp.float32)
    qr = rotary(q.reshape(B, L, RET_HEADS, RET_DK), pos)
    kr = rotary(k.reshape(B, L, RET_HEADS, RET_DK), pos) * (RET_DK ** -0.5)
    o, S_new = retention(qr, kr, v.reshape(B, L, RET_HEADS, RET_DV), S0)
    mu = jnp.mean(o, axis=-1, keepdims=True)
    var = jnp.mean(jnp.square(o - mu), axis=-1, keepdims=True)
    o = ((o - mu) * lax.rsqrt(var + GN_EPS)).reshape(B, L, RET_W) * g_gn.astype(jnp.float32)
    o = o * jax.nn.silu(gr.astype(jnp.float32))
    mix = jnp.concatenate([y_pool, o], axis=-1).astype(h.dtype)
    h = h + mix @ w_out
    gate = jax.nn.sigmoid((rmsnorm(h, g_ple) @ w_pg).astype(jnp.float32))
    h = h + (gate * (p_i @ w_ple).astype(jnp.float32)).astype(h.dtype)
    return h, new_hist, S_new


def trunk(x, p, hist0, S0, offset, g_mix, w_in, w_pool, pool_scale, g_gn, w_out, g_ple, w_pg, w_ple, g_final):
    h = x
    hists, states = [], []
    for i in range(DEPTH):
        h, nh, ns = layer(h, p[i], hist0[i], S0[i], offset, g_mix[i], w_in[i], w_pool[i],
                          pool_scale[i], g_gn[i], w_out[i], g_ple[i], w_pg[i], w_ple[i])
        hists.append(nh)
        states.append(ns)
    return rmsnorm(h, g_final), jnp.stack(hists, axis=0), jnp.stack(states, axis=0)


def setup_inputs(seed: int = 0) -> dict:
    key = jax.random.key(seed)
    ks = jax.random.split(key, 20)
    f32 = jnp.float32
    nrm = lambda k, s, sc: jax.random.normal(k, s, f32) * sc
    return {
        'x_prompt': nrm(ks[0], (BATCH, SEQ, D_MODEL), 1.0),
        'x_sample': nrm(ks[1], (DEC_BATCH, DEC_SEQ, D_MODEL), 1.0),
        'p_prompt': nrm(ks[2], (DEPTH, BATCH, SEQ, PLE_DIM), 1.0),
        'p_sample': nrm(ks[3], (DEPTH, DEC_BATCH, DEC_SEQ, PLE_DIM), 1.0),
        'state_pool': nrm(ks[4], (DEPTH, DEC_BATCH, POOL_HIST, POOL_W), 1.0),
        'state_ret': nrm(ks[5], (DEPTH, DEC_BATCH, RET_HEADS, RET_DK, RET_DV), 0.1),
        'g_mix': 1.0 + nrm(ks[6], (DEPTH, D_MODEL), 0.02),
        'w_in': nrm(ks[7], (DEPTH, D_MODEL, IN_W), D_MODEL ** -0.5),
        'w_pool': nrm(ks[8], (DEPTH, POOL_GROUPS, POOL_GW, POOL_GW), POOL_GW ** -0.5),
        'pool_scale': 1.0 + nrm(ks[9], (DEPTH, POOL_W), 0.02),
        'g_gn': 1.0 + nrm(ks[10], (DEPTH, RET_W), 0.02),
        'w_out': nrm(ks[11], (DEPTH, MIX_W, D_MODEL), MIX_W ** -0.5),
        'g_ple': 1.0 + nrm(ks[12], (DEPTH, D_MODEL), 0.02),
        'w_pg': nrm(ks[13], (DEPTH, D_MODEL, D_MODEL), D_MODEL ** -0.5),
        'w_ple': nrm(ks[14], (DEPTH, PLE_DIM, D_MODEL), PLE_DIM ** -0.5),
        'g_final': 1.0 + nrm(ks[15], (D_MODEL,), 0.02),
    }


def reference(x_prompt, x_sample, p_prompt, p_sample, state_pool, state_ret, g_mix, w_in, w_pool,
              pool_scale, g_gn, w_out, g_ple, w_pg, w_ple, g_final):
    hist0 = jnp.zeros((DEPTH, BATCH, POOL_HIST, POOL_W), x_prompt.dtype)
    S0 = jnp.zeros((DEPTH, BATCH, RET_HEADS, RET_DK, RET_DV), x_prompt.dtype)
    y_prompt, new_pool_prompt, new_ret_prompt = trunk(
        x_prompt, p_prompt, hist0, S0, 0, g_mix, w_in, w_pool, pool_scale, g_gn, w_out,
        g_ple, w_pg, w_ple, g_final)
    y_sample, new_pool_sample, new_ret_sample = trunk(
        x_sample, p_sample, state_pool, state_ret, PAST_LEN, g_mix, w_in, w_pool, pool_scale,
        g_gn, w_out, g_ple, w_pg, w_ple, g_final)
    return (y_prompt, y_sample, new_pool_prompt, new_ret_prompt, new_pool_sample, new_ret_sample)
```

```python
import functools

import jax
import jax.numpy as jnp
from jax import lax
from jax.experimental import pallas as pl
from jax.experimental.pallas import tpu as pltpu

D_MODEL = 2048
DEPTH = 4
PLE_DIM = 256
POOL_W = 1024
POOL_WINDOWS = (2, 4, 8, 16)
POOL_GW = POOL_W // len(POOL_WINDOWS)
POOL_HIST = max(POOL_WINDOWS) - 1
HIST_ROWS = POOL_HIST + 1
RET_HEADS = 4
RET_DK = 256
RET_DV = 256
RET_W = RET_HEADS * RET_DV
MIX_W = POOL_W + RET_W
IN_W = 2 * POOL_W + 2 * RET_HEADS * RET_DK + 2 * RET_W
ROPE_BASE = 10000.0
EPS = 1e-6
GN_EPS = 1e-5
HALF = RET_DK // 2

ROW_TILE = 256
OUT_ROW_TILE = 512
VMEM_LIMIT_BYTES = 56 * 1024 * 1024

F32 = jnp.float32
BF16 = jnp.bfloat16


def _resident(block_shape, index_map):
    return pl.BlockSpec(block_shape, index_map, pipeline_mode=pl.Buffered(1))


def _rmsnorm(x, g):
    ms = jnp.mean(x * x, axis=-1, keepdims=True)
    return x * lax.rsqrt(ms + EPS) * g


def _silu(x):
    return x * jax.nn.sigmoid(x)


def _dot(a, b):
    return jnp.dot(a, b, preferred_element_type=F32)


def _dot_nt(a, b):
    return lax.dot_general(a, b, (((1,), (1,)), ((), ())), preferred_element_type=F32)


def _dot_tn(a, b):
    return lax.dot_general(a, b, (((0,), (0,)), ((), ())), preferred_element_type=F32)


def _rotate(x, cos, sin):
    x1, x2 = x[:, :HALF], x[:, HALF:]
    return jnp.concatenate([x1 * cos - x2 * sin, x1 * sin + x2 * cos], axis=-1)


def _both_halves(ref, hd):
    return jnp.concatenate([ref[hd], ref[hd]], axis=-1)


def _pool_branch(ubuf_ref, gp, wpool_ref, pscale_ref, mix_ref, pos, seq_rows):
    n_seq = ubuf_ref.shape[0]
    rows = n_seq * seq_rows
    for g, w in enumerate(POOL_WINDOWS):
        cols = slice(g * POOL_GW, (g + 1) * POOL_GW)
        u_g = ubuf_ref[:, HIST_ROWS:HIST_ROWS + seq_rows, cols]
        acc = u_g
        for j in range(1, w):
            acc = acc + ubuf_ref[:, HIST_ROWS - j:HIST_ROWS - j + seq_rows, cols]
        acc = acc.reshape(rows, POOL_GW)
        u_g = u_g.reshape(rows, POOL_GW)
        cnt = jnp.minimum(pos + 1, w).astype(F32)
        pooled = acc / cnt - u_g
        y = _dot(pooled.astype(BF16), wpool_ref[g])
        y = y * pscale_ref[:, cols] * _silu(gp[:, cols])
        mix_ref[:, cols] = y.astype(mix_ref.dtype)


def _group_norm_gate(o, ggn, gate):
    mu = jnp.mean(o, axis=-1, keepdims=True)
    d = o - mu
    var = jnp.mean(d * d, axis=-1, keepdims=True)
    return d * lax.rsqrt(var + GN_EPS) * ggn * gate


def _mixer_prompt_kernel(h_ref, cos_ref, sin_ref, dmat_ref, qdec_ref, kdec_ref, cdec_ref,
                         gmix_ref, win_ref, wpool_ref, pscale_ref, ggn_ref,
                         mix_ref, hist_ref, s_ref, ubuf_ref, *, offset):
    T = h_ref.shape[0]
    t = pl.program_id(1)

    @pl.when(t == 0)
    def _():
        ubuf_ref[:, :HIST_ROWS, :] = jnp.zeros((1, HIST_ROWS, POOL_W), F32)
        s_ref[...] = jnp.zeros(s_ref.shape, F32)

    hn = _rmsnorm(h_ref[...], gmix_ref[...]).astype(BF16)

    def proj(j):
        return _dot(hn, win_ref[:, j * POOL_W:(j + 1) * POOL_W])

    ubuf_ref[0, HIST_ROWS:, :] = proj(0)
    gp = proj(1)
    pos = offset + t * T + lax.broadcasted_iota(jnp.int32, (T, POOL_GW), 0)
    _pool_branch(ubuf_ref, gp, wpool_ref, pscale_ref, mix_ref, pos, T)

    @pl.when(t == pl.num_programs(1) - 1)
    def _():
        hist_ref[...] = ubuf_ref[0, T + 1:T + HIST_ROWS, :]

    ubuf_ref[0, :HIST_ROWS, :] = ubuf_ref[0, T:T + HIST_ROWS, :]

    q, k, v, gr = proj(2), proj(3), proj(4).astype(BF16), proj(5)
    cos, sin = cos_ref[...], sin_ref[...]
    for hd in range(RET_HEADS):
        cols = slice(hd * RET_DK, (hd + 1) * RET_DK)
        qr = _rotate(q[:, cols], cos, sin).astype(BF16)
        kr = _rotate(k[:, cols], cos, sin) * (RET_DK ** -0.5)
        kd = (kr * _both_halves(kdec_ref, hd)).astype(BF16)
        vh = v[:, cols]
        p = (_dot_nt(qr, kr.astype(BF16)) * dmat_ref[hd]).astype(BF16)
        s_old = s_ref[hd]
        o = _dot(p, vh) + _dot(qr, s_old.astype(BF16)) * _both_halves(qdec_ref, hd)
        s_ref[hd] = s_old * cdec_ref[hd] + _dot_tn(kd, vh)
        on = _group_norm_gate(o, ggn_ref[:, cols], _silu(gr[:, cols]))
        mix_ref[:, POOL_W + hd * RET_DV:POOL_W + (hd + 1) * RET_DV] = on.astype(mix_ref.dtype)


def _mixer_sample_kernel(h_ref, cos_ref, sin_ref, dmat_ref, qdec_ref, kdec_ref, cdec_ref,
                         hist0_ref, s0_ref, gmix_ref, win_ref, wpool_ref, pscale_ref, ggn_ref,
                         mix_ref, hist_ref, s_ref,
                         ubuf_ref, qr_ref, kdt_ref, v_ref, oin_ref, ox_ref, gate_ref,
                         *, offset, seq_rows):
    rows = h_ref.shape[0]
    s = pl.program_id(0)

    @pl.when(s == 0)
    def _():
        n_seq = rows // seq_rows
        hn = _rmsnorm(h_ref[...], gmix_ref[...]).astype(BF16)

        def proj(j):
            return _dot(hn, win_ref[:, j * POOL_W:(j + 1) * POOL_W])

        ubuf_ref[:, :HIST_ROWS, :] = hist0_ref[...]
        ubuf_ref[:, HIST_ROWS:, :] = proj(0).reshape(n_seq, seq_rows, POOL_W)
        gp = proj(1)
        row = lax.broadcasted_iota(jnp.int32, (rows, POOL_GW), 0)
        pos = offset + lax.rem(row, seq_rows)
        _pool_branch(ubuf_ref, gp, wpool_ref, pscale_ref, mix_ref, pos, seq_rows)
        hist_ref[...] = ubuf_ref[:, seq_rows + 1:seq_rows + HIST_ROWS, :]

        q, k = proj(2), proj(3)
        v_ref[...] = proj(4).astype(BF16)
        gate_ref[...] = _silu(proj(5))
        cos, sin = cos_ref[...], sin_ref[...]
        for hd in range(RET_HEADS):
            cols = slice(hd * RET_DK, (hd + 1) * RET_DK)
            qr = _rotate(q[:, cols], cos, sin).astype(BF16)
            kr = _rotate(k[:, cols], cos, sin) * (RET_DK ** -0.5)
            qr_ref[:, cols] = qr
            kdt_ref[hd] = (kr * _both_halves(kdec_ref, hd)).T.astype(BF16)
            p = (_dot_nt(qr, kr.astype(BF16)) * dmat_ref[hd]).astype(BF16)
            oin_ref[:, cols] = _dot(p, v_ref[:, cols])

    r0 = pl.multiple_of(s * seq_rows, seq_rows)
    row_id = lax.broadcasted_iota(jnp.int32, (rows, RET_DV), 0)
    in_seq = (row_id >= r0) & (row_id < r0 + seq_rows)
    for hd in range(RET_HEADS):
        cols = slice(hd * RET_DK, (hd + 1) * RET_DK)
        s_old = s0_ref[hd]
        ox_ref[pl.ds(r0, seq_rows), cols] = _dot(qr_ref[pl.ds(r0, seq_rows), cols],
                                                 s_old.astype(BF16))
        vh = v_ref[:, cols]
        v_seq = jnp.where(in_seq, vh, jnp.zeros_like(vh))
        s_ref[hd] = s_old * cdec_ref[hd] + _dot(kdt_ref[hd], v_seq)

    @pl.when(s == pl.num_programs(0) - 1)
    def _():
        for hd in range(RET_HEADS):
            cols = slice(hd * RET_DK, (hd + 1) * RET_DK)
            o = oin_ref[:, cols] + ox_ref[:, cols] * _both_halves(qdec_ref, hd)
            on = _group_norm_gate(o, ggn_ref[:, cols], gate_ref[:, cols])
            mix_ref[:, POOL_W + hd * RET_DV:POOL_W + (hd + 1) * RET_DV] = on.astype(mix_ref.dtype)


def _output_kernel(h_ref, mix_ref, p_ref, wout_ref, gple_ref, wpg_ref, wple_ref, gfin_ref,
                   out_ref, *, final_norm):
    h = h_ref[...] + _dot(mix_ref[...], wout_ref[...])
    hn = _rmsnorm(h, gple_ref[...]).astype(BF16)
    gate = jax.nn.sigmoid(_dot(hn, wpg_ref[...]))
    h = h + gate * _dot(p_ref[...].astype(BF16), wple_ref[...])
    if final_norm:
        h = _rmsnorm(h, gfin_ref[...])
    out_ref[...] = h


def _retention_tables(rows, seq_rows):
    lg = jnp.log(1.0 - 2.0 ** (-5.0 - jnp.arange(RET_HEADS, dtype=F32)))
    n = jnp.arange(rows)
    r = (n % seq_rows).astype(F32)
    same = (n // seq_rows)[:, None] == (n // seq_rows)[None, :]
    diff = r[:, None] - r[None, :]
    keep = same & (diff >= 0)
    dmat = jnp.where(keep[None], jnp.exp(jnp.maximum(diff, 0.0)[None] * lg[:, None, None]), 0.0)
    qdec = jnp.exp((r[None, :] + 1.0) * lg[:, None])
    kdec = jnp.exp((seq_rows - 1.0 - r[None, :]) * lg[:, None])
    cdec = jnp.exp(seq_rows * lg)
    bcast = lambda a: jnp.broadcast_to(a[:, :, None], (RET_HEADS, rows, HALF))
    return dmat, bcast(qdec), bcast(kdec), cdec


def _rope_tables(pos):
    inv = 1.0 / (ROPE_BASE ** (jnp.arange(HALF, dtype=F32) / HALF))
    ang = pos.astype(F32)[:, None] * inv[None, :]
    return jnp.cos(ang), jnp.sin(ang)


def _mixer_prompt(h, layer, offset, tables, rope, w):
    B, L, _ = h.shape
    T = ROW_TILE
    dmat, qdec, kdec, cdec = tables
    cos, sin = rope
    const = lambda *idx: (lambda b, t: idx)
    in_specs = [
        pl.BlockSpec((None, T, D_MODEL), lambda b, t: (b, t, 0)),
        pl.BlockSpec((T, HALF), lambda b, t: (t, 0)),
        pl.BlockSpec((T, HALF), lambda b, t: (t, 0)),
        _resident((RET_HEADS, T, T), const(0, 0, 0)),
        _resident((RET_HEADS, T, HALF), const(0, 0, 0)),
        _resident((RET_HEADS, T, HALF), const(0, 0, 0)),
        pl.BlockSpec(memory_space=pltpu.SMEM),
        _resident((None, 1, D_MODEL), const(layer, 0, 0)),
        _resident((None, D_MODEL, IN_W), const(layer, 0, 0)),
        _resident((None, len(POOL_WINDOWS), POOL_GW, POOL_GW), const(layer, 0, 0, 0)),
        _resident((None, 1, POOL_W), const(layer, 0, 0)),
        _resident((None, 1, RET_W), const(layer, 0, 0)),
    ]
    out_specs = [
        pl.BlockSpec((None, T, MIX_W), lambda b, t: (b, t, 0)),
        pl.BlockSpec((None, POOL_HIST, POOL_W), lambda b, t: (b, 0, 0)),
        pl.BlockSpec((None, RET_HEADS, RET_DK, RET_DV), lambda b, t: (b, 0, 0, 0)),
    ]
    out_shape = [
        jax.ShapeDtypeStruct((B, L, MIX_W), BF16),
        jax.ShapeDtypeStruct((B, POOL_HIST, POOL_W), F32),
        jax.ShapeDtypeStruct((B, RET_HEADS, RET_DK, RET_DV), F32),
    ]
    return pl.pallas_call(
        functools.partial(_mixer_prompt_kernel, offset=offset),
        grid=(B, L // T),
        in_specs=in_specs,
        out_specs=out_specs,
        out_shape=out_shape,
        scratch_shapes=[pltpu.VMEM((1, HIST_ROWS + T, POOL_W), F32)],
        compiler_params=pltpu.CompilerParams(
            dimension_semantics=("arbitrary", "arbitrary"),
            vmem_limit_bytes=VMEM_LIMIT_BYTES),
        name=f"mixer_prompt_l{layer}",
    )(h, cos, sin, dmat, qdec, kdec, cdec, w["g_mix"], w["w_in"], w["w_pool"],
      w["pool_scale"], w["g_gn"])


def _mixer_sample(h2d, hist0, s0, layer, offset, seq_rows, tables, rope, w):
    rows = h2d.shape[0]
    n_seq = rows // seq_rows
    dmat, qdec, kdec, cdec = tables
    cos, sin = rope
    const = lambda *idx: (lambda s: idx)
    in_specs = [
        _resident((rows, D_MODEL), const(0, 0)),
        _resident((rows, HALF), const(0, 0)),
        _resident((rows, HALF), const(0, 0)),
        _resident((RET_HEADS, rows, rows), const(0, 0, 0)),
        _resident((RET_HEADS, rows, HALF), const(0, 0, 0)),
        _resident((RET_HEADS, rows, HALF), const(0, 0, 0)),
        pl.BlockSpec(memory_space=pltpu.SMEM),
        _resident((None, n_seq, HIST_ROWS, POOL_W), const(layer, 0, 0, 0)),
        pl.BlockSpec((None, None, RET_HEADS, RET_DK, RET_DV), lambda s: (layer, s, 0, 0, 0)),
        _resident((None, 1, D_MODEL), const(layer, 0, 0)),
        _resident((None, D_MODEL, IN_W), const(layer, 0, 0)),
        _resident((None, len(POOL_WINDOWS), POOL_GW, POOL_GW), const(layer, 0, 0, 0)),
        _resident((None, 1, POOL_W), const(layer, 0, 0)),
        _resident((None, 1, RET_W), const(layer, 0, 0)),
    ]
    out_specs = [
        pl.BlockSpec((rows, MIX_W), const(0, 0)),
        pl.BlockSpec((n_seq, POOL_HIST, POOL_W), const(0, 0, 0)),
        pl.BlockSpec((None, RET_HEADS, RET_DK, RET_DV), lambda s: (s, 0, 0, 0)),
    ]
    out_shape = [
        jax.ShapeDtypeStruct((rows, MIX_W), BF16),
        jax.ShapeDtypeStruct((n_seq, POOL_HIST, POOL_W), F32),
        jax.ShapeDtypeStruct((n_seq, RET_HEADS, RET_DK, RET_DV), F32),
    ]
    scratch_shapes = [
        pltpu.VMEM((n_seq, HIST_ROWS + seq_rows, POOL_W), F32),
        pltpu.VMEM((rows, RET_W), BF16),
        pltpu.VMEM((RET_HEADS, RET_DK, rows), BF16),
        pltpu.VMEM((rows, RET_W), BF16),
        pltpu.VMEM((rows, RET_W), F32),
        pltpu.VMEM((rows, RET_W), F32),
        pltpu.VMEM((rows, RET_W), F32),
    ]
    return pl.pallas_call(
        functools.partial(_mixer_sample_kernel, offset=offset, seq_rows=seq_rows),
        grid=(n_seq,),
        in_specs=in_specs,
        out_specs=out_specs,
        out_shape=out_shape,
        scratch_shapes=scratch_shapes,
        compiler_params=pltpu.CompilerParams(
            dimension_semantics=("arbitrary",),
            vmem_limit_bytes=VMEM_LIMIT_BYTES),
        name=f"mixer_sample_l{layer}",
    )(h2d, cos, sin, dmat, qdec, kdec, cdec, hist0, s0, w["g_mix"], w["w_in"], w["w_pool"],
      w["pool_scale"], w["g_gn"])


def _output_call(h2d, mix2d, p2d, layer, w, tag):
    rows = h2d.shape[0]
    T = min(OUT_ROW_TILE, rows)
    const = lambda *idx: (lambda i: idx)
    in_specs = [
        pl.BlockSpec((T, D_MODEL), lambda i: (i, 0)),
        pl.BlockSpec((T, MIX_W), lambda i: (i, 0)),
        pl.BlockSpec((None, T, PLE_DIM), lambda i: (layer, i, 0)),
        _resident((None, MIX_W, D_MODEL), const(layer, 0, 0)),
        _resident((None, 1, D_MODEL), const(layer, 0, 0)),
        _resident((None, D_MODEL, D_MODEL), const(layer, 0, 0)),
        _resident((None, PLE_DIM, D_MODEL), const(layer, 0, 0)),
        _resident((1, D_MODEL), const(0, 0)),
    ]
    return pl.pallas_call(
        functools.partial(_output_kernel, final_norm=(layer == DEPTH - 1)),
        grid=(rows // T,),
        in_specs=in_specs,
        out_specs=pl.BlockSpec((T, D_MODEL), lambda i: (i, 0)),
        out_shape=jax.ShapeDtypeStruct((rows, D_MODEL), F32),
        compiler_params=pltpu.CompilerParams(
            dimension_semantics=("arbitrary",),
            vmem_limit_bytes=VMEM_LIMIT_BYTES),
        name=f"output_{tag}_l{layer}",
    )(h2d, mix2d, p2d, w["w_out"], w["g_ple"], w["w_pg"], w["w_ple"], w["g_final"])


def _prompt_trunk(x, p, w):
    B, L, _ = x.shape
    tables = _retention_tables(ROW_TILE, ROW_TILE)
    rope = _rope_tables(jnp.arange(L))
    p2d = p.reshape(DEPTH, B * L, PLE_DIM)
    h = x
    hists, states = [], []
    for layer in range(DEPTH):
        mix, hist, state = _mixer_prompt(h, layer, 0, tables, rope, w)
        h = _output_call(h.reshape(B * L, D_MODEL), mix.reshape(B * L, MIX_W), p2d, layer, w,
                         "prompt").reshape(B, L, D_MODEL)
        hists.append(hist)
        states.append(state)
    return h, jnp.stack(hists, axis=0), jnp.stack(states, axis=0)


def _sample_trunk(x, p, hist0, s0, offset, w):
    B, L, _ = x.shape
    rows = B * L
    tables = _retention_tables(rows, L)
    rope = _rope_tables(offset + jnp.arange(rows) % L)
    p2d = p.reshape(DEPTH, rows, PLE_DIM)
    hist0 = jnp.pad(hist0, ((0, 0), (0, 0), (HIST_ROWS - POOL_HIST, 0), (0, 0)))
    h = x.reshape(rows, D_MODEL)
    hists, states = [], []
    for layer in range(DEPTH):
        mix, hist, state = _mixer_sample(h, hist0, s0, layer, offset, L, tables, rope, w)
        h = _output_call(h, mix, p2d, layer, w, "sample")
        hists.append(hist)
        states.append(state)
    return h.reshape(B, L, D_MODEL), jnp.stack(hists, axis=0), jnp.stack(states, axis=0)


def kernel(x_prompt, x_sample, p_prompt, p_sample, state_pool, state_ret, g_mix, w_in, w_pool,
           pool_scale, g_gn, w_out, g_ple, w_pg, w_ple, g_final):
    past_len = 1024
    row = lambda a: a.reshape(a.shape[0], 1, a.shape[1])
    w = {
        "g_mix": row(g_mix), "w_in": w_in.astype(BF16), "w_pool": w_pool.astype(BF16),
        "pool_scale": row(pool_scale), "g_gn": row(g_gn), "w_out": w_out.astype(BF16),
        "g_ple": row(g_ple), "w_pg": w_pg.astype(BF16), "w_ple": w_ple.astype(BF16),
        "g_final": g_final.reshape(1, D_MODEL),
    }
    y_prompt, pool_prompt, ret_prompt = _prompt_trunk(x_prompt, p_prompt, w)
    y_sample, pool_sample, ret_sample = _sample_trunk(
        x_sample, p_sample, state_pool, state_ret, past_len, w)
    return (y_prompt, y_sample, pool_prompt, ret_prompt, pool_sample, ret_sample)
```

```python
import functools

import jax
import jax.numpy as jnp
import numpy as np
from jax import lax
from jax.experimental import pallas as pl
from jax.experimental.pallas import tpu as pltpu

D_MODEL = 2048
DEPTH = 4
PLE_DIM = 256
POOL_W = 1024
POOL_WINDOWS = (2, 4, 8, 16)
POOL_GW = POOL_W // len(POOL_WINDOWS)
POOL_HIST = max(POOL_WINDOWS) - 1
HIST_ROWS = POOL_HIST + 1
RET_HEADS = 4
RET_DK = 256
RET_DV = 256
RET_W = RET_HEADS * RET_DV
MIX_W = POOL_W + RET_W
IN_W = 2 * POOL_W + 2 * RET_HEADS * RET_DK + 2 * RET_W
ROPE_BASE = 10000.0
EPS = 1e-6
GN_EPS = 1e-5
HALF = RET_DK // 2

RET_CHUNK = 256
ROW_TILE = 512
OUT_ROW_TILE = 512
SEQS_PER_STEP = 2
VMEM_LIMIT_BYTES = 56 * 1024 * 1024

F32 = jnp.float32
BF16 = jnp.bfloat16


def _resident(block_shape, index_map):
    return pl.BlockSpec(block_shape, index_map, pipeline_mode=pl.Buffered(1))


def _rmsnorm(x, g):
    ms = jnp.mean(x * x, axis=-1, keepdims=True)
    return x * lax.rsqrt(ms + EPS) * g


def _silu(x):
    return x * jax.nn.sigmoid(x)


def _dot(a, b):
    return jnp.dot(a, b, preferred_element_type=F32)


def _dot_nt(a, b):
    return lax.dot_general(a, b, (((1,), (1,)), ((), ())), preferred_element_type=F32)


def _dot_tn(a, b):
    return lax.dot_general(a, b, (((0,), (0,)), ((), ())), preferred_element_type=F32)


def _rotate(x, cos, sin):
    x1, x2 = x[:, :HALF], x[:, HALF:]
    return jnp.concatenate([x1 * cos - x2 * sin, x1 * sin + x2 * cos], axis=-1)


def _both_halves(ref, hd):
    return jnp.concatenate([ref[hd], ref[hd]], axis=-1)


def _pool_branch(ubuf_ref, gp, wpool_ref, pscale_ref, mix_ref, pos, seq_rows):
    n_seq = ubuf_ref.shape[0]
    rows = n_seq * seq_rows
    for g, w in enumerate(POOL_WINDOWS):
        cols = slice(g * POOL_GW, (g + 1) * POOL_GW)
        u_g = ubuf_ref[:, HIST_ROWS:HIST_ROWS + seq_rows, cols]
        acc = u_g
        for j in range(1, w):
            acc = acc + ubuf_ref[:, HIST_ROWS - j:HIST_ROWS - j + seq_rows, cols]
        acc = acc.reshape(rows, POOL_GW)
        u_g = u_g.reshape(rows, POOL_GW)
        cnt = jnp.minimum(pos + 1, w).astype(F32)
        pooled = acc / cnt - u_g
        y = _dot(pooled.astype(BF16), wpool_ref[g])
        y = y * pscale_ref[:, cols] * _silu(gp[:, cols])
        mix_ref[:, cols] = y.astype(mix_ref.dtype)


def _group_norm_gate(o, ggn, gate):
    mu = jnp.mean(o, axis=-1, keepdims=True)
    d = o - mu
    var = jnp.mean(d * d, axis=-1, keepdims=True)
    return d * lax.rsqrt(var + GN_EPS) * ggn * gate


def _mixer_prompt_kernel(h_ref, cos_ref, sin_ref, dmat_ref, qdec_ref, kdec_ref, cdec_ref,
                         gmix_ref, win_ref, wpool_ref, pscale_ref, ggn_ref,
                         mix_ref, hist_ref, s_ref, ubuf_ref, *, offset):
    T = h_ref.shape[0]
    C = dmat_ref.shape[1]
    t = pl.program_id(1)

    @pl.when(t == 0)
    def _():
        ubuf_ref[:, :HIST_ROWS, :] = jnp.zeros((1, HIST_ROWS, POOL_W), F32)
        s_ref[...] = jnp.zeros(s_ref.shape, F32)

    for c in range(T // C):
        rows = slice(c * C, (c + 1) * C)
        hn = _rmsnorm(h_ref[rows, :], gmix_ref[...]).astype(BF16)

        def proj(j, hn=hn):
            return _dot(hn, win_ref[:, j * POOL_W:(j + 1) * POOL_W])

        ubuf_ref[0, HIST_ROWS + c * C:HIST_ROWS + (c + 1) * C, :] = proj(0)
        gp = proj(1)
        pos = offset + t * T + c * C + lax.broadcasted_iota(jnp.int32, (C, POOL_GW), 0)
        _pool_branch(ubuf_ref.at[:, c * C:HIST_ROWS + (c + 1) * C, :], gp, wpool_ref, pscale_ref,
                     mix_ref.at[rows, :], pos, C)

        q, k, v, gr = proj(2), proj(3), proj(4).astype(BF16), proj(5)
        cos, sin = cos_ref[rows, :], sin_ref[rows, :]
        for hd in range(RET_HEADS):
            cols = slice(hd * RET_DK, (hd + 1) * RET_DK)
            qr = _rotate(q[:, cols], cos, sin).astype(BF16)
            kr = _rotate(k[:, cols], cos, sin) * (RET_DK ** -0.5)
            kd = (kr * _both_halves(kdec_ref, hd)).astype(BF16)
            vh = v[:, cols]
            p = (_dot_nt(qr, kr.astype(BF16)) * dmat_ref[hd]).astype(BF16)
            s_old = s_ref[hd]
            o = _dot(p, vh) + _dot(qr, s_old.astype(BF16)) * _both_halves(qdec_ref, hd)
            s_ref[hd] = s_old * cdec_ref[hd] + _dot_tn(kd, vh)
            on = _group_norm_gate(o, ggn_ref[:, cols], _silu(gr[:, cols]))
            mix_ref[rows, POOL_W + hd * RET_DV:POOL_W + (hd + 1) * RET_DV] = on.astype(mix_ref.dtype)

    @pl.when(t == pl.num_programs(1) - 1)
    def _():
        hist_ref[...] = ubuf_ref[0, T + 1:T + HIST_ROWS, :]

    ubuf_ref[0, :HIST_ROWS, :] = ubuf_ref[0, T:T + HIST_ROWS, :]


def _mixer_sample_kernel(h_ref, cos_ref, sin_ref, dmat_ref, qdec_ref, kdec_ref, cdec_ref,
                         hist0_ref, s0_ref, gmix_ref, win_ref, wpool_ref, pscale_ref, ggn_ref,
                         mix_ref, hist_ref, s_ref,
                         ubuf_ref, qr_ref, kdt_ref, v_ref, oin_ref, ox_ref, gate_ref,
                         *, offset, seq_rows):
    rows = h_ref.shape[0]
    s = pl.program_id(0)

    @pl.when(s == 0)
    def _():
        n_seq = rows // seq_rows
        hn = _rmsnorm(h_ref[...], gmix_ref[...]).astype(BF16)

        def proj(j):
            return _dot(hn, win_ref[:, j * POOL_W:(j + 1) * POOL_W])

        ubuf_ref[:, :HIST_ROWS, :] = hist0_ref[...]
        ubuf_ref[:, HIST_ROWS:, :] = proj(0).reshape(n_seq, seq_rows, POOL_W)
        gp = proj(1)
        row = lax.broadcasted_iota(jnp.int32, (rows, POOL_GW), 0)
        pos = offset + lax.rem(row, seq_rows)
        _pool_branch(ubuf_ref, gp, wpool_ref, pscale_ref, mix_ref, pos, seq_rows)
        hist_ref[...] = ubuf_ref[:, seq_rows + 1:seq_rows + HIST_ROWS, :]

        q, k = proj(2), proj(3)
        v_ref[...] = proj(4).astype(BF16)
        gate_ref[...] = _silu(proj(5))
        cos, sin = cos_ref[...], sin_ref[...]
        for hd in range(RET_HEADS):
            cols = slice(hd * RET_DK, (hd + 1) * RET_DK)
            qr = _rotate(q[:, cols], cos, sin).astype(BF16)
            kr = _rotate(k[:, cols], cos, sin) * (RET_DK ** -0.5)
            qr_ref[:, cols] = qr
            kdt_ref[hd] = (kr * _both_halves(kdec_ref, hd)).T.astype(BF16)
            p = (_dot_nt(qr, kr.astype(BF16)) * dmat_ref[hd]).astype(BF16)
            oin_ref[:, cols] = _dot(p, v_ref[:, cols])

    row_id = lax.broadcasted_iota(jnp.int32, (rows, RET_DV), 0)
    for i in range(s0_ref.shape[0]):
        r0 = pl.multiple_of((s * s0_ref.shape[0] + i) * seq_rows, seq_rows)
        in_seq = (row_id >= r0) & (row_id < r0 + seq_rows)
        for hd in range(RET_HEADS):
            cols = slice(hd * RET_DK, (hd + 1) * RET_DK)
            s_old = s0_ref[i, hd]
            ox_ref[pl.ds(r0, seq_rows), cols] = _dot(qr_ref[pl.ds(r0, seq_rows), cols],
                                                     s_old.astype(BF16))
            vh = v_ref[:, cols]
            v_seq = jnp.where(in_seq, vh, jnp.zeros_like(vh))
            s_ref[i, hd] = s_old * cdec_ref[hd] + _dot(kdt_ref[hd], v_seq)

    @pl.when(s == pl.num_programs(0) - 1)
    def _():
        for hd in range(RET_HEADS):
            cols = slice(hd * RET_DK, (hd + 1) * RET_DK)
            o = oin_ref[:, cols] + ox_ref[:, cols] * _both_halves(qdec_ref, hd)
            on = _group_norm_gate(o, ggn_ref[:, cols], gate_ref[:, cols])
            mix_ref[:, POOL_W + hd * RET_DV:POOL_W + (hd + 1) * RET_DV] = on.astype(mix_ref.dtype)


def _output_kernel(h_ref, mix_ref, p_ref, wout_ref, gple_ref, wpg_ref, wple_ref, gfin_ref,
                   out_ref, *, final_norm):
    h = h_ref[...] + _dot(mix_ref[...], wout_ref[...])
    hn = _rmsnorm(h, gple_ref[...]).astype(BF16)
    gate = jax.nn.sigmoid(_dot(hn, wpg_ref[...]))
    h = h + gate * _dot(p_ref[...].astype(BF16), wple_ref[...])
    if final_norm:
        h = _rmsnorm(h, gfin_ref[...])
    out_ref[...] = h


def _retention_tables(rows, seq_rows):
    lg = np.log(1.0 - 2.0 ** (-5.0 - np.arange(RET_HEADS, dtype=np.float64)))
    n = np.arange(rows)
    r = (n % seq_rows).astype(np.float64)
    same = (n // seq_rows)[:, None] == (n // seq_rows)[None, :]
    diff = r[:, None] - r[None, :]
    keep = same & (diff >= 0)
    dmat = np.where(keep[None], np.exp(np.maximum(diff, 0.0)[None] * lg[:, None, None]), 0.0)
    qdec = np.exp((r[None, :] + 1.0) * lg[:, None])
    kdec = np.exp((seq_rows - 1.0 - r[None, :]) * lg[:, None])
    cdec = np.exp(seq_rows * lg)
    bcast = lambda a: np.broadcast_to(a[:, :, None], (RET_HEADS, rows, HALF))
    return tuple(jnp.asarray(a, F32) for a in (dmat, bcast(qdec), bcast(kdec), cdec))


def _rope_tables(pos):
    inv = 1.0 / (ROPE_BASE ** (jnp.arange(HALF, dtype=F32) / HALF))
    ang = pos.astype(F32)[:, None] * inv[None, :]
    return jnp.cos(ang), jnp.sin(ang)


def _mixer_prompt(h, layer, offset, tables, rope, w):
    B, L, _ = h.shape
    T = ROW_TILE
    dmat, qdec, kdec, cdec = tables
    cos, sin = rope
    const = lambda *idx: (lambda b, t: idx)
    in_specs = [
        pl.BlockSpec((None, T, D_MODEL), lambda b, t: (b, t, 0)),
        pl.BlockSpec((T, HALF), lambda b, t: (t, 0)),
        pl.BlockSpec((T, HALF), lambda b, t: (t, 0)),
        _resident((RET_HEADS, RET_CHUNK, RET_CHUNK), const(0, 0, 0)),
        _resident((RET_HEADS, RET_CHUNK, HALF), const(0, 0, 0)),
        _resident((RET_HEADS, RET_CHUNK, HALF), const(0, 0, 0)),
        pl.BlockSpec(memory_space=pltpu.SMEM),
        _resident((None, 1, D_MODEL), const(layer, 0, 0)),
        _resident((None, D_MODEL, IN_W), const(layer, 0, 0)),
        _resident((None, len(POOL_WINDOWS), POOL_GW, POOL_GW), const(layer, 0, 0, 0)),
        _resident((None, 1, POOL_W), const(layer, 0, 0)),
        _resident((None, 1, RET_W), const(layer, 0, 0)),
    ]
    out_specs = [
        pl.BlockSpec((None, T, MIX_W), lambda b, t: (b, t, 0)),
        pl.BlockSpec((None, POOL_HIST, POOL_W), lambda b, t: (b, 0, 0)),
        pl.BlockSpec((None, RET_HEADS, RET_DK, RET_DV), lambda b, t: (b, 0, 0, 0)),
    ]
    out_shape = [
        jax.ShapeDtypeStruct((B, L, MIX_W), BF16),
        jax.ShapeDtypeStruct((B, POOL_HIST, POOL_W), F32),
        jax.ShapeDtypeStruct((B, RET_HEADS, RET_DK, RET_DV), F32),
    ]
    return pl.pallas_call(
        functools.partial(_mixer_prompt_kernel, offset=offset),
        grid=(B, L // T),
        in_specs=in_specs,
        out_specs=out_specs,
        out_shape=out_shape,
        scratch_shapes=[pltpu.VMEM((1, HIST_ROWS + T, POOL_W), F32)],
        compiler_params=pltpu.CompilerParams(
            dimension_semantics=("arbitrary", "arbitrary"),
            vmem_limit_bytes=VMEM_LIMIT_BYTES),
        name=f"mixer_prompt_l{layer}",
    )(h, cos, sin, dmat, qdec, kdec, cdec, w["g_mix"], w["w_in"], w["w_pool"],
      w["pool_scale"], w["g_gn"])


def _mixer_sample(h2d, hist0, s0, layer, offset, seq_rows, tables, rope, w):
    rows = h2d.shape[0]
    n_seq = rows // seq_rows
    dmat, qdec, kdec, cdec = tables
    cos, sin = rope
    const = lambda *idx: (lambda s: idx)
    in_specs = [
        _resident((rows, D_MODEL), const(0, 0)),
        _resident((rows, HALF), const(0, 0)),
        _resident((rows, HALF), const(0, 0)),
        _resident((RET_HEADS, rows, rows), const(0, 0, 0)),
        _resident((RET_HEADS, rows, HALF), const(0, 0, 0)),
        _resident((RET_HEADS, rows, HALF), const(0, 0, 0)),
        pl.BlockSpec(memory_space=pltpu.SMEM),
        _resident((None, n_seq, HIST_ROWS, POOL_W), const(layer, 0, 0, 0)),
        pl.BlockSpec((None, SEQS_PER_STEP, RET_HEADS, RET_DK, RET_DV),
                     lambda s: (layer, s, 0, 0, 0)),
        _resident((None, 1, D_MODEL), const(layer, 0, 0)),
        _resident((None, D_MODEL, IN_W), const(layer, 0, 0)),
        _resident((None, len(POOL_WINDOWS), POOL_GW, POOL_GW), const(layer, 0, 0, 0)),
        _resident((None, 1, POOL_W), const(layer, 0, 0)),
        _resident((None, 1, RET_W), const(layer, 0, 0)),
    ]
    out_specs = [
        pl.BlockSpec((rows, MIX_W), const(0, 0)),
        pl.BlockSpec((n_seq, POOL_HIST, POOL_W), const(0, 0, 0)),
        pl.BlockSpec((SEQS_PER_STEP, RET_HEADS, RET_DK, RET_DV), lambda s: (s, 0, 0, 0)),
    ]
    out_shape = [
        jax.ShapeDtypeStruct((rows, MIX_W), BF16),
        jax.ShapeDtypeStruct((n_seq, POOL_HIST, POOL_W), F32),
        jax.ShapeDtypeStruct((n_seq, RET_HEADS, RET_DK, RET_DV), F32),
    ]
    scratch_shapes = [
        pltpu.VMEM((n_seq, HIST_ROWS + seq_rows, POOL_W), F32),
        pltpu.VMEM((rows, RET_W), BF16),
        pltpu.VMEM((RET_HEADS, RET_DK, rows), BF16),
        pltpu.VMEM((rows, RET_W), BF16),
        pltpu.VMEM((rows, RET_W), F32),
        pltpu.VMEM((rows, RET_W), F32),
        pltpu.VMEM((rows, RET_W), F32),
    ]
    return pl.pallas_call(
        functools.partial(_mixer_sample_kernel, offset=offset, seq_rows=seq_rows),
        grid=(n_seq // SEQS_PER_STEP,),
        in_specs=in_specs,
        out_specs=out_specs,
        out_shape=out_shape,
        scratch_shapes=scratch_shapes,
        compiler_params=pltpu.CompilerParams(
            dimension_semantics=("arbitrary",),
            vmem_limit_bytes=VMEM_LIMIT_BYTES),
        name=f"mixer_sample_l{layer}",
    )(h2d, cos, sin, dmat, qdec, kdec, cdec, hist0, s0, w["g_mix"], w["w_in"], w["w_pool"],
      w["pool_scale"], w["g_gn"])


def _output_call(h2d, mix2d, p2d, layer, w, tag):
    rows = h2d.shape[0]
    T = min(OUT_ROW_TILE, rows)
    const = lambda *idx: (lambda i: idx)
    in_specs = [
        pl.BlockSpec((T, D_MODEL), lambda i: (i, 0)),
        pl.BlockSpec((T, MIX_W), lambda i: (i, 0)),
        pl.BlockSpec((None, T, PLE_DIM), lambda i: (layer, i, 0)),
        _resident((None, MIX_W, D_MODEL), const(layer, 0, 0)),
        _resident((None, 1, D_MODEL), const(layer, 0, 0)),
        _resident((None, D_MODEL, D_MODEL), const(layer, 0, 0)),
        _resident((None, PLE_DIM, D_MODEL), const(layer, 0, 0)),
        _resident((1, D_MODEL), const(0, 0)),
    ]
    return pl.pallas_call(
        functools.partial(_output_kernel, final_norm=(layer == DEPTH - 1)),
        grid=(rows // T,),
        in_specs=in_specs,
        out_specs=pl.BlockSpec((T, D_MODEL), lambda i: (i, 0)),
        out_shape=jax.ShapeDtypeStruct((rows, D_MODEL), F32),
        compiler_params=pltpu.CompilerParams(
            dimension_semantics=("arbitrary",),
            vmem_limit_bytes=VMEM_LIMIT_BYTES),
        name=f"output_{tag}_l{layer}",
    )(h2d, mix2d, p2d, w["w_out"], w["g_ple"], w["w_pg"], w["w_ple"], w["g_final"])


def _prompt_trunk(x, p, w):
    B, L, _ = x.shape
    tables = _retention_tables(RET_CHUNK, RET_CHUNK)
    rope = _rope_tables(jnp.arange(L))
    p2d = p.reshape(DEPTH, B * L, PLE_DIM)
    h = x
    hists, states = [], []
    for layer in range(DEPTH):
        mix, hist, state = _mixer_prompt(h, layer, 0, tables, rope, w)
        h = _output_call(h.reshape(B * L, D_MODEL), mix.reshape(B * L, MIX_W), p2d, layer, w,
                         "prompt").reshape(B, L, D_MODEL)
        hists.append(hist)
        states.append(state)
    return h, jnp.stack(hists, axis=0), jnp.stack(states, axis=0)


def _sample_trunk(x, p, hist0, s0, offset, w):
    B, L, _ = x.shape
    rows = B * L
    tables = _retention_tables(rows, L)
    rope = _rope_tables(offset + jnp.arange(rows) % L)
    p2d = p.reshape(DEPTH, rows, PLE_DIM)
    hist0 = jnp.pad(hist0, ((0, 0), (0, 0), (HIST_ROWS - POOL_HIST, 0), (0, 0)))
    h = x.reshape(rows, D_MODEL)
    hists, states = [], []
    for layer in range(DEPTH):
        mix, hist, state = _mixer_sample(h, hist0, s0, layer, offset, L, tables, rope, w)
        h = _output_call(h, mix, p2d, layer, w, "sample")
        hists.append(hist)
        states.append(state)
    return h.reshape(B, L, D_MODEL), jnp.stack(hists, axis=0), jnp.stack(states, axis=0)


def kernel(x_prompt, x_sample, p_prompt, p_sample, state_pool, state_ret, g_mix, w_in, w_pool,
           pool_scale, g_gn, w_out, g_ple, w_pg, w_ple, g_final):
    past_len = 1024
    row = lambda a: a.reshape(a.shape[0], 1, a.shape[1])
    w = {
        "g_mix": row(g_mix), "w_in": w_in.astype(BF16), "w_pool": w_pool.astype(BF16),
        "pool_scale": row(pool_scale), "g_gn": row(g_gn), "w_out": w_out.astype(BF16),
        "g_ple": row(g_ple), "w_pg": w_pg.astype(BF16), "w_ple": w_ple.astype(BF16),
        "g_final": g_final.reshape(1, D_MODEL),
    }
    y_prompt, pool_prompt, ret_prompt = _prompt_trunk(x_prompt, p_prompt, w)
    y_sample, pool_sample, ret_sample = _sample_trunk(
        x_sample, p_sample, state_pool, state_ret, past_len, w)
    return (y_prompt, y_sample, pool_prompt, ret_prompt, pool_sample, ret_sample)
```

```python
import functools

import jax
import jax.numpy as jnp
import numpy as np
from jax import lax
from jax.experimental import pallas as pl
from jax.experimental.pallas import tpu as pltpu

D_MODEL = 2048
DEPTH = 4
PLE_DIM = 256
POOL_W = 1024
POOL_WINDOWS = (2, 4, 8, 16)
POOL_GW = POOL_W // len(POOL_WINDOWS)
POOL_HIST = max(POOL_WINDOWS) - 1
HIST_ROWS = POOL_HIST + 1
RET_HEADS = 4
RET_DK = 256
RET_DV = 256
RET_W = RET_HEADS * RET_DV
MIX_W = POOL_W + RET_W
IN_W = 2 * POOL_W + 2 * RET_HEADS * RET_DK + 2 * RET_W
ROPE_BASE = 10000.0
EPS = 1e-6
GN_EPS = 1e-5
HALF = RET_DK // 2

RET_CHUNK = 256
ROW_TILE = 512
OUT_ROW_TILE = 512
WEIGHT_CHUNK_ROWS = 64
SEQS_PER_STEP = 2
VMEM_LIMIT_BYTES = 56 * 1024 * 1024

F32 = jnp.float32
BF16 = jnp.bfloat16


def _resident(block_shape, index_map):
    return pl.BlockSpec(block_shape, index_map, pipeline_mode=pl.Buffered(1))


def _rmsnorm(x, g):
    ms = jnp.mean(x * x, axis=-1, keepdims=True)
    return x * lax.rsqrt(ms + EPS) * g


def _silu(x):
    return x * jax.nn.sigmoid(x)


def _dot(a, b):
    return jnp.dot(a, b, preferred_element_type=F32)


def _dot_nt(a, b):
    return lax.dot_general(a, b, (((1,), (1,)), ((), ())), preferred_element_type=F32)


def _dot_tn(a, b):
    return lax.dot_general(a, b, (((0,), (0,)), ((), ())), preferred_element_type=F32)


def _rotate(x, cos, sin):
    x1, x2 = x[:, :HALF], x[:, HALF:]
    return jnp.concatenate([x1 * cos - x2 * sin, x1 * sin + x2 * cos], axis=-1)


def _both_halves(ref, hd):
    return jnp.concatenate([ref[hd], ref[hd]], axis=-1)


def _pool_branch(ubuf_ref, gp, wpool_ref, pscale_ref, mix_ref, pos, seq_rows):
    n_seq = ubuf_ref.shape[0]
    rows = n_seq * seq_rows
    for g, w in enumerate(POOL_WINDOWS):
        cols = slice(g * POOL_GW, (g + 1) * POOL_GW)
        u_g = ubuf_ref[:, HIST_ROWS:HIST_ROWS + seq_rows, cols]
        acc = u_g
        for j in range(1, w):
            acc = acc + ubuf_ref[:, HIST_ROWS - j:HIST_ROWS - j + seq_rows, cols]
        acc = acc.reshape(rows, POOL_GW)
        u_g = u_g.reshape(rows, POOL_GW)
        cnt = jnp.minimum(pos + 1, w).astype(F32)
        pooled = acc / cnt - u_g
        y = _dot(pooled.astype(BF16), wpool_ref[g])
        y = y * pscale_ref[:, cols] * _silu(gp[:, cols])
        mix_ref[:, cols] = y.astype(mix_ref.dtype)


def _group_norm_gate(o, ggn, gate):
    mu = jnp.mean(o, axis=-1, keepdims=True)
    d = o - mu
    var = jnp.mean(d * d, axis=-1, keepdims=True)
    return d * lax.rsqrt(var + GN_EPS) * ggn * gate


def _weight_chunk_copy(w_hbm, layer, stage_ref, sem_ref, c, slot):
    rows = stage_ref.shape[1]
    r0 = pl.multiple_of(c * rows, rows)
    return pltpu.make_async_copy(w_hbm.at[layer, pl.ds(r0, rows), :], stage_ref.at[slot],
                                 sem_ref.at[slot])


def _stream_weight_bf16(w_hbm, layer, dst_ref, stage_ref, sem_ref):
    rows = stage_ref.shape[1]
    n_chunks = dst_ref.shape[0] // rows
    _weight_chunk_copy(w_hbm, layer, stage_ref, sem_ref, 0, 0).start()

    def body(c, carry):
        slot = lax.rem(c, 2)

        @pl.when(c + 1 < n_chunks)
        def _():
            _weight_chunk_copy(w_hbm, layer, stage_ref, sem_ref, c + 1, 1 - slot).start()

        _weight_chunk_copy(w_hbm, layer, stage_ref, sem_ref, c, slot).wait()
        r0 = pl.multiple_of(c * rows, rows)
        dst_ref[pl.ds(r0, rows), :] = stage_ref[slot].astype(dst_ref.dtype)
        return carry

    lax.fori_loop(0, n_chunks, body, 0)


def _mixer_prompt_kernel(h_ref, cos_ref, sin_ref, dmat_ref, qdec_ref, kdec_ref, cdec_ref,
                         gmix_ref, win_hbm, wpool_ref, pscale_ref, ggn_ref,
                         mix_ref, hist_ref, s_ref, win_bf16_hbm,
                         ubuf_ref, win_ref, stage_ref, stage_sem, export_sem, *, offset, layer):
    T = h_ref.shape[0]
    C = dmat_ref.shape[1]
    t = pl.program_id(1)
    first = (pl.program_id(0) == 0) & (t == 0)
    last = (pl.program_id(0) == pl.num_programs(0) - 1) & (t == pl.num_programs(1) - 1)
    export = pltpu.make_async_copy(win_ref, win_bf16_hbm, export_sem.at[0])

    @pl.when(first)
    def _():
        _stream_weight_bf16(win_hbm, layer, win_ref, stage_ref, stage_sem)
        export.start()

    @pl.when(t == 0)
    def _():
        ubuf_ref[:, :HIST_ROWS, :] = jnp.zeros((1, HIST_ROWS, POOL_W), F32)
        s_ref[...] = jnp.zeros(s_ref.shape, F32)

    for c in range(T // C):
        rows = slice(c * C, (c + 1) * C)
        hn = _rmsnorm(h_ref[rows, :], gmix_ref[...]).astype(BF16)

        def proj(j, hn=hn):
            return _dot(hn, win_ref[:, j * POOL_W:(j + 1) * POOL_W])

        ubuf_ref[0, HIST_ROWS + c * C:HIST_ROWS + (c + 1) * C, :] = proj(0)
        gp = proj(1)
        pos = offset + t * T + c * C + lax.broadcasted_iota(jnp.int32, (C, POOL_GW), 0)
        _pool_branch(ubuf_ref.at[:, c * C:HIST_ROWS + (c + 1) * C, :], gp, wpool_ref, pscale_ref,
                     mix_ref.at[rows, :], pos, C)

        q, k, v, gr = proj(2), proj(3), proj(4).astype(BF16), proj(5)
        cos, sin = cos_ref[rows, :], sin_ref[rows, :]
        for hd in range(RET_HEADS):
            cols = slice(hd * RET_DK, (hd + 1) * RET_DK)
            qr = _rotate(q[:, cols], cos, sin).astype(BF16)
            kr = _rotate(k[:, cols], cos, sin) * (RET_DK ** -0.5)
            kd = (kr * _both_halves(kdec_ref, hd)).astype(BF16)
            vh = v[:, cols]
            p = (_dot_nt(qr, kr.astype(BF16)) * dmat_ref[hd]).astype(BF16)
            s_old = s_ref[hd]
            o = _dot(p, vh) + _dot(qr, s_old.astype(BF16)) * _both_halves(qdec_ref, hd)
            s_ref[hd] = s_old * cdec_ref[hd] + _dot_tn(kd, vh)
            on = _group_norm_gate(o, ggn_ref[:, cols], _silu(gr[:, cols]))
            mix_ref[rows, POOL_W + hd * RET_DV:POOL_W + (hd + 1) * RET_DV] = on.astype(mix_ref.dtype)

    @pl.when(t == pl.num_programs(1) - 1)
    def _():
        hist_ref[...] = ubuf_ref[0, T + 1:T + HIST_ROWS, :]

    ubuf_ref[0, :HIST_ROWS, :] = ubuf_ref[0, T:T + HIST_ROWS, :]

    @pl.when(last)
    def _():
        export.wait()


def _mixer_sample_kernel(h_ref, cos_ref, sin_ref, dmat_ref, qdec_ref, kdec_ref, cdec_ref,
                         hist0_ref, s0_ref, gmix_ref, win_ref, wpool_ref, pscale_ref, ggn_ref,
                         mix_ref, hist_ref, s_ref,
                         ubuf_ref, qr_ref, kdt_ref, v_ref, oin_ref, ox_ref, gate_ref,
                         *, offset, seq_rows):
    rows = h_ref.shape[0]
    s = pl.program_id(0)

    @pl.when(s == 0)
    def _():
        n_seq = rows // seq_rows
        hn = _rmsnorm(h_ref[...], gmix_ref[...]).astype(BF16)

        def proj(j):
            return _dot(hn, win_ref[:, j * POOL_W:(j + 1) * POOL_W])

        ubuf_ref[:, :HIST_ROWS, :] = hist0_ref[...]
        ubuf_ref[:, HIST_ROWS:, :] = proj(0).reshape(n_seq, seq_rows, POOL_W)
        gp = proj(1)
        row = lax.broadcasted_iota(jnp.int32, (rows, POOL_GW), 0)
        pos = offset + lax.rem(row, seq_rows)
        _pool_branch(ubuf_ref, gp, wpool_ref, pscale_ref, mix_ref, pos, seq_rows)
        hist_ref[...] = ubuf_ref[:, seq_rows + 1:seq_rows + HIST_ROWS, :]

        q, k = proj(2), proj(3)
        v_ref[...] = proj(4).astype(BF16)
        gate_ref[...] = _silu(proj(5))
        cos, sin = cos_ref[...], sin_ref[...]
        for hd in range(RET_HEADS):
            cols = slice(hd * RET_DK, (hd + 1) * RET_DK)
            qr = _rotate(q[:, cols], cos, sin).astype(BF16)
            kr = _rotate(k[:, cols], cos, sin) * (RET_DK ** -0.5)
            qr_ref[:, cols] = qr
            kdt_ref[hd] = (kr * _both_halves(kdec_ref, hd)).T.astype(BF16)
            p = (_dot_nt(qr, kr.astype(BF16)) * dmat_ref[hd]).astype(BF16)
            oin_ref[:, cols] = _dot(p, v_ref[:, cols])

    row_id = lax.broadcasted_iota(jnp.int32, (rows, RET_DV), 0)
    for i in range(s0_ref.shape[0]):
        r0 = pl.multiple_of((s * s0_ref.shape[0] + i) * seq_rows, seq_rows)
        in_seq = (row_id >= r0) & (row_id < r0 + seq_rows)
        for hd in range(RET_HEADS):
            cols = slice(hd * RET_DK, (hd + 1) * RET_DK)
            s_old = s0_ref[i, hd]
            ox_ref[pl.ds(r0, seq_rows), cols] = _dot(qr_ref[pl.ds(r0, seq_rows), cols],
                                                     s_old.astype(BF16))
            vh = v_ref[:, cols]
            v_seq = jnp.where(in_seq, vh, jnp.zeros_like(vh))
            s_ref[i, hd] = s_old * cdec_ref[hd] + _dot(kdt_ref[hd], v_seq)

    @pl.when(s == pl.num_programs(0) - 1)
    def _():
        for hd in range(RET_HEADS):
            cols = slice(hd * RET_DK, (hd + 1) * RET_DK)
            o = oin_ref[:, cols] + ox_ref[:, cols] * _both_halves(qdec_ref, hd)
            on = _group_norm_gate(o, ggn_ref[:, cols], gate_ref[:, cols])
            mix_ref[:, POOL_W + hd * RET_DV:POOL_W + (hd + 1) * RET_DV] = on.astype(mix_ref.dtype)


def _output_body(h_ref, mix_ref, p_ref, wout_ref, gple_ref, wpg_ref, wple_ref, gfin_ref,
                 out_ref, final_norm):
    h = h_ref[...] + _dot(mix_ref[...], wout_ref[...])
    hn = _rmsnorm(h, gple_ref[...]).astype(BF16)
    gate = jax.nn.sigmoid(_dot(hn, wpg_ref[...]))
    h = h + gate * _dot(p_ref[...].astype(BF16), wple_ref[...])
    if final_norm:
        h = _rmsnorm(h, gfin_ref[...])
    out_ref[...] = h


def _output_sample_kernel(h_ref, mix_ref, p_ref, wout_ref, gple_ref, wpg_ref, wple_ref, gfin_ref,
                          out_ref, *, final_norm):
    _output_body(h_ref, mix_ref, p_ref, wout_ref, gple_ref, wpg_ref, wple_ref, gfin_ref,
                 out_ref, final_norm)


def _output_prompt_kernel(h_ref, mix_ref, p_ref, wout_hbm, gple_ref, wpg_hbm, wple_ref, gfin_ref,
                          out_ref, wout_bf16_hbm, wpg_bf16_hbm,
                          wout_ref, wpg_ref, stage_ref, stage_sem, export_sem,
                          *, final_norm, layer):
    i = pl.program_id(0)
    exports = [pltpu.make_async_copy(wout_ref, wout_bf16_hbm, export_sem.at[0]),
               pltpu.make_async_copy(wpg_ref, wpg_bf16_hbm, export_sem.at[1])]

    @pl.when(i == 0)
    def _():
        _stream_weight_bf16(wout_hbm, layer, wout_ref, stage_ref, stage_sem)
        exports[0].start()
        _stream_weight_bf16(wpg_hbm, layer, wpg_ref, stage_ref, stage_sem)
        exports[1].start()

    _output_body(h_ref, mix_ref, p_ref, wout_ref, gple_ref, wpg_ref, wple_ref, gfin_ref,
                 out_ref, final_norm)

    @pl.when(i == pl.num_programs(0) - 1)
    def _():
        for e in exports:
            e.wait()


def _retention_tables(rows, seq_rows):
    lg = np.log(1.0 - 2.0 ** (-5.0 - np.arange(RET_HEADS, dtype=np.float64)))
    n = np.arange(rows)
    r = (n % seq_rows).astype(np.float64)
    same = (n // seq_rows)[:, None] == (n // seq_rows)[None, :]
    diff = r[:, None] - r[None, :]
    keep = same & (diff >= 0)
    dmat = np.where(keep[None], np.exp(np.maximum(diff, 0.0)[None] * lg[:, None, None]), 0.0)
    qdec = np.exp((r[None, :] + 1.0) * lg[:, None])
    kdec = np.exp((seq_rows - 1.0 - r[None, :]) * lg[:, None])
    cdec = np.exp(seq_rows * lg)
    bcast = lambda a: np.broadcast_to(a[:, :, None], (RET_HEADS, rows, HALF))
    return tuple(jnp.asarray(a, F32) for a in (dmat, bcast(qdec), bcast(kdec), cdec))


def _rope_tables(pos):
    inv = 1.0 / (ROPE_BASE ** (jnp.arange(HALF, dtype=F32) / HALF))
    ang = pos.astype(F32)[:, None] * inv[None, :]
    return jnp.cos(ang), jnp.sin(ang)


def _mixer_prompt(h, layer, offset, tables, rope, w):
    B, L, _ = h.shape
    T = ROW_TILE
    dmat, qdec, kdec, cdec = tables
    cos, sin = rope
    const = lambda *idx: (lambda b, t: idx)
    in_specs = [
        pl.BlockSpec((None, T, D_MODEL), lambda b, t: (b, t, 0)),
        pl.BlockSpec((T, HALF), lambda b, t: (t, 0)),
        pl.BlockSpec((T, HALF), lambda b, t: (t, 0)),
        _resident((RET_HEADS, RET_CHUNK, RET_CHUNK), const(0, 0, 0)),
        _resident((RET_HEADS, RET_CHUNK, HALF), const(0, 0, 0)),
        _resident((RET_HEADS, RET_CHUNK, HALF), const(0, 0, 0)),
        pl.BlockSpec(memory_space=pltpu.SMEM),
        _resident((None, 1, D_MODEL), const(layer, 0, 0)),
        pl.BlockSpec(memory_space=pl.ANY),
        _resident((None, len(POOL_WINDOWS), POOL_GW, POOL_GW), const(layer, 0, 0, 0)),
        _resident((None, 1, POOL_W), const(layer, 0, 0)),
        _resident((None, 1, RET_W), const(layer, 0, 0)),
    ]
    out_specs = [
        pl.BlockSpec((None, T, MIX_W), lambda b, t: (b, t, 0)),
        pl.BlockSpec((None, POOL_HIST, POOL_W), lambda b, t: (b, 0, 0)),
        pl.BlockSpec((None, RET_HEADS, RET_DK, RET_DV), lambda b, t: (b, 0, 0, 0)),
        pl.BlockSpec(memory_space=pl.ANY),
    ]
    out_shape = [
        jax.ShapeDtypeStruct((B, L, MIX_W), BF16),
        jax.ShapeDtypeStruct((B, POOL_HIST, POOL_W), F32),
        jax.ShapeDtypeStruct((B, RET_HEADS, RET_DK, RET_DV), F32),
        jax.ShapeDtypeStruct((D_MODEL, IN_W), BF16),
    ]
    scratch_shapes = [
        pltpu.VMEM((1, HIST_ROWS + T, POOL_W), F32),
        pltpu.VMEM((D_MODEL, IN_W), BF16),
        pltpu.VMEM((2, WEIGHT_CHUNK_ROWS, IN_W), F32),
        pltpu.SemaphoreType.DMA((2,)),
        pltpu.SemaphoreType.DMA((1,)),
    ]
    return pl.pallas_call(
        functools.partial(_mixer_prompt_kernel, offset=offset, layer=layer),
        grid=(B, L // T),
        in_specs=in_specs,
        out_specs=out_specs,
        out_shape=out_shape,
        scratch_shapes=scratch_shapes,
        compiler_params=pltpu.CompilerParams(
            dimension_semantics=("arbitrary", "arbitrary"),
            vmem_limit_bytes=VMEM_LIMIT_BYTES),
        name=f"mixer_prompt_l{layer}",
    )(h, cos, sin, dmat, qdec, kdec, cdec, w["g_mix"], w["w_in"], w["w_pool"],
      w["pool_scale"], w["g_gn"])


def _mixer_sample(h2d, hist0, s0, layer, offset, seq_rows, tables, rope, w, win_bf16):
    rows = h2d.shape[0]
    n_seq = rows // seq_rows
    dmat, qdec, kdec, cdec = tables
    cos, sin = rope
    const = lambda *idx: (lambda s: idx)
    in_specs = [
        _resident((rows, D_MODEL), const(0, 0)),
        _resident((rows, HALF), const(0, 0)),
        _resident((rows, HALF), const(0, 0)),
        _resident((RET_HEADS, rows, rows), const(0, 0, 0)),
        _resident((RET_HEADS, rows, HALF), const(0, 0, 0)),
        _resident((RET_HEADS, rows, HALF), const(0, 0, 0)),
        pl.BlockSpec(memory_space=pltpu.SMEM),
        _resident((None, n_seq, HIST_ROWS, POOL_W), const(layer, 0, 0, 0)),
        pl.BlockSpec((None, SEQS_PER_STEP, RET_HEADS, RET_DK, RET_DV),
                     lambda s: (layer, s, 0, 0, 0)),
        _resident((None, 1, D_MODEL), const(layer, 0, 0)),
        _resident((D_MODEL, IN_W), const(0, 0)),
        _resident((None, len(POOL_WINDOWS), POOL_GW, POOL_GW), const(layer, 0, 0, 0)),
        _resident((None, 1, POOL_W), const(layer, 0, 0)),
        _resident((None, 1, RET_W), const(layer, 0, 0)),
    ]
    out_specs = [
        pl.BlockSpec((rows, MIX_W), const(0, 0)),
        pl.BlockSpec((n_seq, POOL_HIST, POOL_W), const(0, 0, 0)),
        pl.BlockSpec((SEQS_PER_STEP, RET_HEADS, RET_DK, RET_DV), lambda s: (s, 0, 0, 0)),
    ]
    out_shape = [
        jax.ShapeDtypeStruct((rows, MIX_W), BF16),
        jax.ShapeDtypeStruct((n_seq, POOL_HIST, POOL_W), F32),
        jax.ShapeDtypeStruct((n_seq, RET_HEADS, RET_DK, RET_DV), F32),
    ]
    scratch_shapes = [
        pltpu.VMEM((n_seq, HIST_ROWS + seq_rows, POOL_W), F32),
        pltpu.VMEM((rows, RET_W), BF16),
        pltpu.VMEM((RET_HEADS, RET_DK, rows), BF16),
        pltpu.VMEM((rows, RET_W), BF16),
        pltpu.VMEM((rows, RET_W), F32),
        pltpu.VMEM((rows, RET_W), F32),
        pltpu.VMEM((rows, RET_W), F32),
    ]
    return pl.pallas_call(
        functools.partial(_mixer_sample_kernel, offset=offset, seq_rows=seq_rows),
        grid=(n_seq // SEQS_PER_STEP,),
        in_specs=in_specs,
        out_specs=out_specs,
        out_shape=out_shape,
        scratch_shapes=scratch_shapes,
        compiler_params=pltpu.CompilerParams(
            dimension_semantics=("arbitrary",),
            vmem_limit_bytes=VMEM_LIMIT_BYTES),
        name=f"mixer_sample_l{layer}",
    )(h2d, cos, sin, dmat, qdec, kdec, cdec, hist0, s0, w["g_mix"], win_bf16, w["w_pool"],
      w["pool_scale"], w["g_gn"])


def _output_prompt_call(h2d, mix2d, p2d, layer, w):
    rows = h2d.shape[0]
    T = OUT_ROW_TILE
    const = lambda *idx: (lambda i: idx)
    in_specs = [
        pl.BlockSpec((T, D_MODEL), lambda i: (i, 0)),
        pl.BlockSpec((T, MIX_W), lambda i: (i, 0)),
        pl.BlockSpec((None, T, PLE_DIM), lambda i: (layer, i, 0)),
        pl.BlockSpec(memory_space=pl.ANY),
        _resident((None, 1, D_MODEL), const(layer, 0, 0)),
        pl.BlockSpec(memory_space=pl.ANY),
        _resident((None, PLE_DIM, D_MODEL), const(layer, 0, 0)),
        _resident((1, D_MODEL), const(0, 0)),
    ]
    out_specs = [
        pl.BlockSpec((T, D_MODEL), lambda i: (i, 0)),
        pl.BlockSpec(memory_space=pl.ANY),
        pl.BlockSpec(memory_space=pl.ANY),
    ]
    out_shape = [
        jax.ShapeDtypeStruct((rows, D_MODEL), F32),
        jax.ShapeDtypeStruct((MIX_W, D_MODEL), BF16),
        jax.ShapeDtypeStruct((D_MODEL, D_MODEL), BF16),
    ]
    scratch_shapes = [
        pltpu.VMEM((MIX_W, D_MODEL), BF16),
        pltpu.VMEM((D_MODEL, D_MODEL), BF16),
        pltpu.VMEM((2, WEIGHT_CHUNK_ROWS, D_MODEL), F32),
        pltpu.SemaphoreType.DMA((2,)),
        pltpu.SemaphoreType.DMA((2,)),
    ]
    return pl.pallas_call(
        functools.partial(_output_prompt_kernel, final_norm=(layer == DEPTH - 1), layer=layer),
        grid=(rows // T,),
        in_specs=in_specs,
        out_specs=out_specs,
        out_shape=out_shape,
        scratch_shapes=scratch_shapes,
        compiler_params=pltpu.CompilerParams(
            dimension_semantics=("arbitrary",),
            vmem_limit_bytes=VMEM_LIMIT_BYTES),
        name=f"output_prompt_l{layer}",
    )(h2d, mix2d, p2d, w["w_out"], w["g_ple"], w["w_pg"], w["w_ple"], w["g_final"])


def _output_sample_call(h2d, mix2d, p2d, layer, w, wout_bf16, wpg_bf16):
    rows = h2d.shape[0]
    const = lambda *idx: (lambda i: idx)
    in_specs = [
        _resident((rows, D_MODEL), const(0, 0)),
        _resident((rows, MIX_W), const(0, 0)),
        _resident((None, rows, PLE_DIM), const(layer, 0, 0)),
        _resident((MIX_W, D_MODEL), const(0, 0)),
        _resident((None, 1, D_MODEL), const(layer, 0, 0)),
        _resident((D_MODEL, D_MODEL), const(0, 0)),
        _resident((None, PLE_DIM, D_MODEL), const(layer, 0, 0)),
        _resident((1, D_MODEL), const(0, 0)),
    ]
    return pl.pallas_call(
        functools.partial(_output_sample_kernel, final_norm=(layer == DEPTH - 1)),
        grid=(1,),
        in_specs=in_specs,
        out_specs=pl.BlockSpec((rows, D_MODEL), const(0, 0)),
        out_shape=jax.ShapeDtypeStruct((rows, D_MODEL), F32),
        compiler_params=pltpu.CompilerParams(
            dimension_semantics=("arbitrary",),
            vmem_limit_bytes=VMEM_LIMIT_BYTES),
        name=f"output_sample_l{layer}",
    )(h2d, mix2d, p2d, wout_bf16, w["g_ple"], wpg_bf16, w["w_ple"], w["g_final"])


def _prompt_trunk(x, p, w):
    B, L, _ = x.shape
    tables = _retention_tables(RET_CHUNK, RET_CHUNK)
    rope = _rope_tables(jnp.arange(L))
    p2d = p.reshape(DEPTH, B * L, PLE_DIM)
    h = x
    hists, states, w_bf16 = [], [], []
    for layer in range(DEPTH):
        mix, hist, state, win_b = _mixer_prompt(h, layer, 0, tables, rope, w)
        h, wout_b, wpg_b = _output_prompt_call(h.reshape(B * L, D_MODEL),
                                               mix.reshape(B * L, MIX_W), p2d, layer, w)
        h = h.reshape(B, L, D_MODEL)
        hists.append(hist)
        states.append(state)
        w_bf16.append((win_b, wout_b, wpg_b))
    return h, jnp.stack(hists, axis=0), jnp.stack(states, axis=0), w_bf16


def _sample_trunk(x, p, hist0, s0, offset, w, w_bf16):
    B, L, _ = x.shape
    rows = B * L
    tables = _retention_tables(rows, L)
    rope = _rope_tables(offset + jnp.arange(rows) % L)
    p2d = p.reshape(DEPTH, rows, PLE_DIM)
    hist0 = jnp.pad(hist0, ((0, 0), (0, 0), (HIST_ROWS - POOL_HIST, 0), (0, 0)))
    h = x.reshape(rows, D_MODEL)
    hists, states = [], []
    for layer in range(DEPTH):
        win_b, wout_b, wpg_b = w_bf16[layer]
        mix, hist, state = _mixer_sample(h, hist0, s0, layer, offset, L, tables, rope, w, win_b)
        h = _output_sample_call(h, mix, p2d, layer, w, wout_b, wpg_b)
        hists.append(hist)
        states.append(state)
    return h.reshape(B, L, D_MODEL), jnp.stack(hists, axis=0), jnp.stack(states, axis=0)


def kernel(x_prompt, x_sample, p_prompt, p_sample, state_pool, state_ret, g_mix, w_in, w_pool,
           pool_scale, g_gn, w_out, g_ple, w_pg, w_ple, g_final):
    past_len = 1024
    row = lambda a: a.reshape(a.shape[0], 1, a.shape[1])
    w = {
        "g_mix": row(g_mix), "w_in": w_in, "w_pool": w_pool.astype(BF16),
        "pool_scale": row(pool_scale), "g_gn": row(g_gn), "w_out": w_out,
        "g_ple": row(g_ple), "w_pg": w_pg, "w_ple": w_ple.astype(BF16),
        "g_final": g_final.reshape(1, D_MODEL),
    }
    y_prompt, pool_prompt, ret_prompt, w_bf16 = _prompt_trunk(x_prompt, p_prompt, w)
    y_sample, pool_sample, ret_sample = _sample_trunk(
        x_sample, p_sample, state_pool, state_ret, past_len, w, w_bf16)
    return (y_prompt, y_sample, pool_prompt, ret_prompt, pool_sample, ret_sample)
```

```python
import functools

import jax
import jax.numpy as jnp
import numpy as np
from jax import lax
from jax.experimental import pallas as pl
from jax.experimental.pallas import tpu as pltpu

D_MODEL = 2048
DEPTH = 4
PLE_DIM = 256
POOL_W = 1024
POOL_WINDOWS = (2, 4, 8, 16)
POOL_GW = POOL_W // len(POOL_WINDOWS)
POOL_HIST = max(POOL_WINDOWS) - 1
HIST_ROWS = POOL_HIST + 1
RET_HEADS = 4
RET_DK = 256
RET_DV = 256
RET_W = RET_HEADS * RET_DV
MIX_W = POOL_W + RET_W
IN_W = 2 * POOL_W + 2 * RET_HEADS * RET_DK + 2 * RET_W
ROPE_BASE = 10000.0
EPS = 1e-6
GN_EPS = 1e-5
HALF = RET_DK // 2

RET_CHUNK = 256
ROW_TILE = 512
OUT_ROW_TILE = 512
IN_STAGE_ROWS = 128
OUT_STAGE_ROWS = 512
SEQS_PER_STEP = 2
VMEM_LIMIT_BYTES = 56 * 1024 * 1024

F32 = jnp.float32
BF16 = jnp.bfloat16


def _resident(block_shape, index_map):
    return pl.BlockSpec(block_shape, index_map, pipeline_mode=pl.Buffered(1))


def _rmsnorm(x, g):
    ms = jnp.mean(x * x, axis=-1, keepdims=True)
    return x * lax.rsqrt(ms + EPS) * g


def _silu(x):
    return x * jax.nn.sigmoid(x)


def _dot(a, b):
    return jnp.dot(a, b, preferred_element_type=F32)


def _dot_nt(a, b):
    return lax.dot_general(a, b, (((1,), (1,)), ((), ())), preferred_element_type=F32)


def _dot_tn(a, b):
    return lax.dot_general(a, b, (((0,), (0,)), ((), ())), preferred_element_type=F32)


def _rotate(x, cos, sin):
    x1, x2 = x[:, :HALF], x[:, HALF:]
    return jnp.concatenate([x1 * cos - x2 * sin, x1 * sin + x2 * cos], axis=-1)


def _both_halves(ref, hd):
    return jnp.concatenate([ref[hd], ref[hd]], axis=-1)


def _pool_branch(ubuf_ref, gp, wpool_ref, pscale_ref, mix_ref, pos, seq_rows):
    n_seq = ubuf_ref.shape[0]
    rows = n_seq * seq_rows
    for g, w in enumerate(POOL_WINDOWS):
        cols = slice(g * POOL_GW, (g + 1) * POOL_GW)
        u_g = ubuf_ref[:, HIST_ROWS:HIST_ROWS + seq_rows, cols]
        acc = u_g
        for j in range(1, w):
            acc = acc + ubuf_ref[:, HIST_ROWS - j:HIST_ROWS - j + seq_rows, cols]
        acc = acc.reshape(rows, POOL_GW)
        u_g = u_g.reshape(rows, POOL_GW)
        cnt = jnp.minimum(pos + 1, w).astype(F32)
        pooled = acc / cnt - u_g
        y = _dot(pooled.astype(BF16), wpool_ref[g])
        y = y * pscale_ref[:, cols] * _silu(gp[:, cols])
        mix_ref[:, cols] = y.astype(mix_ref.dtype)


def _group_norm_gate(o, ggn, gate):
    mu = jnp.mean(o, axis=-1, keepdims=True)
    d = o - mu
    var = jnp.mean(d * d, axis=-1, keepdims=True)
    return d * lax.rsqrt(var + GN_EPS) * ggn * gate


def _weight_chunk_copy(w_hbm, layer, stage_ref, sem_ref, c, slot):
    rows = stage_ref.shape[1]
    r0 = pl.multiple_of(c * rows, rows)
    return pltpu.make_async_copy(w_hbm.at[layer, pl.ds(r0, rows), :], stage_ref.at[slot],
                                 sem_ref.at[slot])


def _stream_weight_bf16(w_hbm, layer, dst_ref, stage_ref, sem_ref):
    rows = stage_ref.shape[1]
    n_chunks = dst_ref.shape[0] // rows
    _weight_chunk_copy(w_hbm, layer, stage_ref, sem_ref, 0, 0).start()

    def body(c, carry):
        slot = lax.rem(c, 2)

        @pl.when(c + 1 < n_chunks)
        def _():
            _weight_chunk_copy(w_hbm, layer, stage_ref, sem_ref, c + 1, 1 - slot).start()

        _weight_chunk_copy(w_hbm, layer, stage_ref, sem_ref, c, slot).wait()
        r0 = pl.multiple_of(c * rows, rows)
        dst_ref[pl.ds(r0, rows), :] = stage_ref[slot].astype(dst_ref.dtype)
        return carry

    lax.fori_loop(0, n_chunks, body, 0)


def _mixer_prompt_kernel(h_ref, cos_ref, sin_ref, dmat_ref, qdec_ref, kdec_ref, cdec_ref,
                         gmix_ref, win_hbm, wpool_ref, pscale_ref, ggn_ref,
                         mix_ref, hist_ref, s_ref, win_bf16_hbm,
                         ubuf_ref, win_ref, stage_ref, stage_sem, export_sem, *, offset, layer):
    T = h_ref.shape[0]
    C = dmat_ref.shape[1]
    t = pl.program_id(1)
    first = (pl.program_id(0) == 0) & (t == 0)
    last = (pl.program_id(0) == pl.num_programs(0) - 1) & (t == pl.num_programs(1) - 1)
    export = pltpu.make_async_copy(win_ref, win_bf16_hbm, export_sem.at[0])

    @pl.when(first)
    def _():
        _stream_weight_bf16(win_hbm, layer, win_ref, stage_ref, stage_sem)
        export.start()

    @pl.when(t == 0)
    def _():
        ubuf_ref[:, :HIST_ROWS, :] = jnp.zeros((1, HIST_ROWS, POOL_W), F32)
        s_ref[...] = jnp.zeros(s_ref.shape, F32)

    for c in range(T // C):
        rows = slice(c * C, (c + 1) * C)
        hn = _rmsnorm(h_ref[rows, :], gmix_ref[...]).astype(BF16)

        def proj(j, hn=hn):
            return _dot(hn, win_ref[:, j * POOL_W:(j + 1) * POOL_W])

        ubuf_ref[0, HIST_ROWS + c * C:HIST_ROWS + (c + 1) * C, :] = proj(0)
        gp = proj(1)
        pos = offset + t * T + c * C + lax.broadcasted_iota(jnp.int32, (C, POOL_GW), 0)
        _pool_branch(ubuf_ref.at[:, c * C:HIST_ROWS + (c + 1) * C, :], gp, wpool_ref, pscale_ref,
                     mix_ref.at[rows, :], pos, C)

        q, k, v, gr = proj(2), proj(3), proj(4).astype(BF16), proj(5)
        cos, sin = cos_ref[rows, :], sin_ref[rows, :]
        for hd in range(RET_HEADS):
            cols = slice(hd * RET_DK, (hd + 1) * RET_DK)
            qr = _rotate(q[:, cols], cos, sin).astype(BF16)
            kr = _rotate(k[:, cols], cos, sin) * (RET_DK ** -0.5)
            kd = (kr * _both_halves(kdec_ref, hd)).astype(BF16)
            vh = v[:, cols]
            p = (_dot_nt(qr, kr.astype(BF16)) * dmat_ref[hd]).astype(BF16)
            s_old = s_ref[hd]
            o = _dot(p, vh) + _dot(qr, s_old.astype(BF16)) * _both_halves(qdec_ref, hd)
            s_ref[hd] = s_old * cdec_ref[hd] + _dot_tn(kd, vh)
            on = _group_norm_gate(o, ggn_ref[:, cols], _silu(gr[:, cols]))
            mix_ref[rows, POOL_W + hd * RET_DV:POOL_W + (hd + 1) * RET_DV] = on.astype(mix_ref.dtype)

    @pl.when(t == pl.num_programs(1) - 1)
    def _():
        hist_ref[...] = ubuf_ref[0, T + 1:T + HIST_ROWS, :]

    ubuf_ref[0, :HIST_ROWS, :] = ubuf_ref[0, T:T + HIST_ROWS, :]

    @pl.when(last)
    def _():
        export.wait()


def _mixer_sample_kernel(h_ref, cos_ref, sin_ref, dmat_ref, qdec_ref, kdec_ref, cdec_ref,
                         hist0_ref, s0_ref, gmix_ref, win_ref, wpool_ref, pscale_ref, ggn_ref,
                         mix_ref, hist_ref, s_ref,
                         ubuf_ref, qr_ref, kdt_ref, v_ref, oin_ref, ox_ref, gate_ref,
                         *, offset, seq_rows):
    rows = h_ref.shape[0]
    s = pl.program_id(0)

    @pl.when(s == 0)
    def _():
        n_seq = rows // seq_rows
        hn = _rmsnorm(h_ref[...], gmix_ref[...]).astype(BF16)

        def proj(j):
            return _dot(hn, win_ref[:, j * POOL_W:(j + 1) * POOL_W])

        ubuf_ref[:, :HIST_ROWS, :] = hist0_ref[...]
        ubuf_ref[:, HIST_ROWS:, :] = proj(0).reshape(n_seq, seq_rows, POOL_W)
        gp = proj(1)
        row = lax.broadcasted_iota(jnp.int32, (rows, POOL_GW), 0)
        pos = offset + lax.rem(row, seq_rows)
        _pool_branch(ubuf_ref, gp, wpool_ref, pscale_ref, mix_ref, pos, seq_rows)
        hist_ref[...] = ubuf_ref[:, seq_rows + 1:seq_rows + HIST_ROWS, :]

        q, k = proj(2), proj(3)
        v_ref[...] = proj(4).astype(BF16)
        gate_ref[...] = _silu(proj(5))
        cos, sin = cos_ref[...], sin_ref[...]
        for hd in range(RET_HEADS):
            cols = slice(hd * RET_DK, (hd + 1) * RET_DK)
            qr = _rotate(q[:, cols], cos, sin).astype(BF16)
            kr = _rotate(k[:, cols], cos, sin) * (RET_DK ** -0.5)
            qr_ref[:, cols] = qr
            kdt_ref[hd] = (kr * _both_halves(kdec_ref, hd)).T.astype(BF16)
            p = (_dot_nt(qr, kr.astype(BF16)) * dmat_ref[hd]).astype(BF16)
            oin_ref[:, cols] = _dot(p, v_ref[:, cols])

    row_id = lax.broadcasted_iota(jnp.int32, (rows, RET_DV), 0)
    for i in range(s0_ref.shape[0]):
        r0 = pl.multiple_of((s * s0_ref.shape[0] + i) * seq_rows, seq_rows)
        in_seq = (row_id >= r0) & (row_id < r0 + seq_rows)
        for hd in range(RET_HEADS):
            cols = slice(hd * RET_DK, (hd + 1) * RET_DK)
            s_old = s0_ref[i, hd]
            ox_ref[pl.ds(r0, seq_rows), cols] = _dot(qr_ref[pl.ds(r0, seq_rows), cols],
                                                     s_old.astype(BF16))
            vh = v_ref[:, cols]
            v_seq = jnp.where(in_seq, vh, jnp.zeros_like(vh))
            s_ref[i, hd] = s_old * cdec_ref[hd] + _dot(kdt_ref[hd], v_seq)

    @pl.when(s == pl.num_programs(0) - 1)
    def _():
        for hd in range(RET_HEADS):
            cols = slice(hd * RET_DK, (hd + 1) * RET_DK)
            o = oin_ref[:, cols] + ox_ref[:, cols] * _both_halves(qdec_ref, hd)
            on = _group_norm_gate(o, ggn_ref[:, cols], gate_ref[:, cols])
            mix_ref[:, POOL_W + hd * RET_DV:POOL_W + (hd + 1) * RET_DV] = on.astype(mix_ref.dtype)


def _output_body(h_ref, mix_ref, p_ref, wout_ref, gple_ref, wpg_ref, wple_ref, gfin_ref,
                 out_ref, final_norm):
    h = h_ref[...] + _dot(mix_ref[...], wout_ref[...])
    hn = _rmsnorm(h, gple_ref[...]).astype(BF16)
    gate = jax.nn.sigmoid(_dot(hn, wpg_ref[...]))
    h = h + gate * _dot(p_ref[...].astype(BF16), wple_ref[...])
    if final_norm:
        h = _rmsnorm(h, gfin_ref[...])
    out_ref[...] = h


def _output_sample_kernel(h_ref, mix_ref, p_ref, wout_ref, gple_ref, wpg_ref, wple_ref, gfin_ref,
                          out_ref, *, final_norm):
    _output_body(h_ref, mix_ref, p_ref, wout_ref, gple_ref, wpg_ref, wple_ref, gfin_ref,
                 out_ref, final_norm)


def _output_prompt_kernel(h_ref, mix_ref, p_ref, wout_hbm, gple_ref, wpg_hbm, wple_ref, gfin_ref,
                          out_ref, wout_bf16_hbm, wpg_bf16_hbm,
                          wout_ref, wpg_ref, stage_ref, stage_sem, export_sem,
                          *, final_norm, layer):
    i = pl.program_id(0)
    exports = [pltpu.make_async_copy(wout_ref, wout_bf16_hbm, export_sem.at[0]),
               pltpu.make_async_copy(wpg_ref, wpg_bf16_hbm, export_sem.at[1])]

    @pl.when(i == 0)
    def _():
        _stream_weight_bf16(wout_hbm, layer, wout_ref, stage_ref, stage_sem)
        exports[0].start()
        _stream_weight_bf16(wpg_hbm, layer, wpg_ref, stage_ref, stage_sem)
        exports[1].start()

    _output_body(h_ref, mix_ref, p_ref, wout_ref, gple_ref, wpg_ref, wple_ref, gfin_ref,
                 out_ref, final_norm)

    @pl.when(i == pl.num_programs(0) - 1)
    def _():
        for e in exports:
            e.wait()


def _retention_tables(rows, seq_rows):
    lg = np.log(1.0 - 2.0 ** (-5.0 - np.arange(RET_HEADS, dtype=np.float64)))
    n = np.arange(rows)
    r = (n % seq_rows).astype(np.float64)
    same = (n // seq_rows)[:, None] == (n // seq_rows)[None, :]
    diff = r[:, None] - r[None, :]
    keep = same & (diff >= 0)
    dmat = np.where(keep[None], np.exp(np.maximum(diff, 0.0)[None] * lg[:, None, None]), 0.0)
    qdec = np.exp((r[None, :] + 1.0) * lg[:, None])
    kdec = np.exp((seq_rows - 1.0 - r[None, :]) * lg[:, None])
    cdec = np.exp(seq_rows * lg)
    bcast = lambda a: np.broadcast_to(a[:, :, None], (RET_HEADS, rows, HALF))
    return tuple(jnp.asarray(a, F32) for a in (dmat, bcast(qdec), bcast(kdec), cdec))


def _rope_tables(pos):
    inv = 1.0 / (ROPE_BASE ** (jnp.arange(HALF, dtype=F32) / HALF))
    ang = pos.astype(F32)[:, None] * inv[None, :]
    return jnp.cos(ang), jnp.sin(ang)


def _mixer_prompt(h, layer, offset, tables, rope, w):
    B, L, _ = h.shape
    T = ROW_TILE
    dmat, qdec, kdec, cdec = tables
    cos, sin = rope
    const = lambda *idx: (lambda b, t: idx)
    in_specs = [
        pl.BlockSpec((None, T, D_MODEL), lambda b, t: (b, t, 0)),
        pl.BlockSpec((T, HALF), lambda b, t: (t, 0)),
        pl.BlockSpec((T, HALF), lambda b, t: (t, 0)),
        _resident((RET_HEADS, RET_CHUNK, RET_CHUNK), const(0, 0, 0)),
        _resident((RET_HEADS, RET_CHUNK, HALF), const(0, 0, 0)),
        _resident((RET_HEADS, RET_CHUNK, HALF), const(0, 0, 0)),
        pl.BlockSpec(memory_space=pltpu.SMEM),
        _resident((None, 1, D_MODEL), const(layer, 0, 0)),
        pl.BlockSpec(memory_space=pl.ANY),
        _resident((None, len(POOL_WINDOWS), POOL_GW, POOL_GW), const(layer, 0, 0, 0)),
        _resident((None, 1, POOL_W), const(layer, 0, 0)),
        _resident((None, 1, RET_W), const(layer, 0, 0)),
    ]
    out_specs = [
        pl.BlockSpec((None, T, MIX_W), lambda b, t: (b, t, 0)),
        pl.BlockSpec((None, POOL_HIST, POOL_W), lambda b, t: (b, 0, 0)),
        pl.BlockSpec((None, RET_HEADS, RET_DK, RET_DV), lambda b, t: (b, 0, 0, 0)),
        pl.BlockSpec(memory_space=pl.ANY),
    ]
    out_shape = [
        jax.ShapeDtypeStruct((B, L, MIX_W), BF16),
        jax.ShapeDtypeStruct((B, POOL_HIST, POOL_W), F32),
        jax.ShapeDtypeStruct((B, RET_HEADS, RET_DK, RET_DV), F32),
        jax.ShapeDtypeStruct((D_MODEL, IN_W), BF16),
    ]
    scratch_shapes = [
        pltpu.VMEM((1, HIST_ROWS + T, POOL_W), F32),
        pltpu.VMEM((D_MODEL, IN_W), BF16),
        pltpu.VMEM((2, IN_STAGE_ROWS, IN_W), F32),
        pltpu.SemaphoreType.DMA((2,)),
        pltpu.SemaphoreType.DMA((1,)),
    ]
    return pl.pallas_call(
        functools.partial(_mixer_prompt_kernel, offset=offset, layer=layer),
        grid=(B, L // T),
        in_specs=in_specs,
        out_specs=out_specs,
        out_shape=out_shape,
        scratch_shapes=scratch_shapes,
        compiler_params=pltpu.CompilerParams(
            dimension_semantics=("arbitrary", "arbitrary"),
            vmem_limit_bytes=VMEM_LIMIT_BYTES),
        name=f"mixer_prompt_l{layer}",
    )(h, cos, sin, dmat, qdec, kdec, cdec, w["g_mix"], w["w_in"], w["w_pool"],
      w["pool_scale"], w["g_gn"])


def _mixer_sample(h2d, hist0, s0, layer, offset, seq_rows, tables, rope, w, win_bf16):
    rows = h2d.shape[0]
    n_seq = rows // seq_rows
    dmat, qdec, kdec, cdec = tables
    cos, sin = rope
    const = lambda *idx: (lambda s: idx)
    in_specs = [
        _resident((rows, D_MODEL), const(0, 0)),
        _resident((rows, HALF), const(0, 0)),
        _resident((rows, HALF), const(0, 0)),
        _resident((RET_HEADS, rows, rows), const(0, 0, 0)),
        _resident((RET_HEADS, rows, HALF), const(0, 0, 0)),
        _resident((RET_HEADS, rows, HALF), const(0, 0, 0)),
        pl.BlockSpec(memory_space=pltpu.SMEM),
        _resident((None, n_seq, HIST_ROWS, POOL_W), const(layer, 0, 0, 0)),
        pl.BlockSpec((None, SEQS_PER_STEP, RET_HEADS, RET_DK, RET_DV),
                     lambda s: (layer, s, 0, 0, 0)),
        _resident((None, 1, D_MODEL), const(layer, 0, 0)),
        _resident((D_MODEL, IN_W), const(0, 0)),
        _resident((None, len(POOL_WINDOWS), POOL_GW, POOL_GW), const(layer, 0, 0, 0)),
        _resident((None, 1, POOL_W), const(layer, 0, 0)),
        _resident((None, 1, RET_W), const(layer, 0, 0)),
    ]
    out_specs = [
        pl.BlockSpec((rows, MIX_W), const(0, 0)),
        pl.BlockSpec((n_seq, POOL_HIST, POOL_W), const(0, 0, 0)),
        pl.BlockSpec((SEQS_PER_STEP, RET_HEADS, RET_DK, RET_DV), lambda s: (s, 0, 0, 0)),
    ]
    out_shape = [
        jax.ShapeDtypeStruct((rows, MIX_W), BF16),
        jax.ShapeDtypeStruct((n_seq, POOL_HIST, POOL_W), F32),
        jax.ShapeDtypeStruct((n_seq, RET_HEADS, RET_DK, RET_DV), F32),
    ]
    scratch_shapes = [
        pltpu.VMEM((n_seq, HIST_ROWS + seq_rows, POOL_W), F32),
        pltpu.VMEM((rows, RET_W), BF16),
        pltpu.VMEM((RET_HEADS, RET_DK, rows), BF16),
        pltpu.VMEM((rows, RET_W), BF16),
        pltpu.VMEM((rows, RET_W), F32),
        pltpu.VMEM((rows, RET_W), F32),
        pltpu.VMEM((rows, RET_W), F32),
    ]
    return pl.pallas_call(
        functools.partial(_mixer_sample_kernel, offset=offset, seq_rows=seq_rows),
        grid=(n_seq // SEQS_PER_STEP,),
        in_specs=in_specs,
        out_specs=out_specs,
        out_shape=out_shape,
        scratch_shapes=scratch_shapes,
        compiler_params=pltpu.CompilerParams(
            dimension_semantics=("arbitrary",),
            vmem_limit_bytes=VMEM_LIMIT_BYTES),
        name=f"mixer_sample_l{layer}",
    )(h2d, cos, sin, dmat, qdec, kdec, cdec, hist0, s0, w["g_mix"], win_bf16, w["w_pool"],
      w["pool_scale"], w["g_gn"])


def _output_prompt_call(h2d, mix2d, p2d, layer, w):
    rows = h2d.shape[0]
    T = OUT_ROW_TILE
    const = lambda *idx: (lambda i: idx)
    in_specs = [
        pl.BlockSpec((T, D_MODEL), lambda i: (i, 0)),
        pl.BlockSpec((T, MIX_W), lambda i: (i, 0)),
        pl.BlockSpec((None, T, PLE_DIM), lambda i: (layer, i, 0)),
        pl.BlockSpec(memory_space=pl.ANY),
        _resident((None, 1, D_MODEL), const(layer, 0, 0)),
        pl.BlockSpec(memory_space=pl.ANY),
        _resident((None, PLE_DIM, D_MODEL), const(layer, 0, 0)),
        _resident((1, D_MODEL), const(0, 0)),
    ]
    out_specs = [
        pl.BlockSpec((T, D_MODEL), lambda i: (i, 0)),
        pl.BlockSpec(memory_space=pl.ANY),
        pl.BlockSpec(memory_space=pl.ANY),
    ]
    out_shape = [
        jax.ShapeDtypeStruct((rows, D_MODEL), F32),
        jax.ShapeDtypeStruct((MIX_W, D_MODEL), BF16),
        jax.ShapeDtypeStruct((D_MODEL, D_MODEL), BF16),
    ]
    scratch_shapes = [
        pltpu.VMEM((MIX_W, D_MODEL), BF16),
        pltpu.VMEM((D_MODEL, D_MODEL), BF16),
        pltpu.VMEM((2, OUT_STAGE_ROWS, D_MODEL), F32),
        pltpu.SemaphoreType.DMA((2,)),
        pltpu.SemaphoreType.DMA((2,)),
    ]
    return pl.pallas_call(
        functools.partial(_output_prompt_kernel, final_norm=(layer == DEPTH - 1), layer=layer),
        grid=(rows // T,),
        in_specs=in_specs,
        out_specs=out_specs,
        out_shape=out_shape,
        scratch_shapes=scratch_shapes,
        compiler_params=pltpu.CompilerParams(
            dimension_semantics=("arbitrary",),
            vmem_limit_bytes=VMEM_LIMIT_BYTES),
        name=f"output_prompt_l{layer}",
    )(h2d, mix2d, p2d, w["w_out"], w["g_ple"], w["w_pg"], w["w_ple"], w["g_final"])


def _output_sample_call(h2d, mix2d, p2d, layer, w, wout_bf16, wpg_bf16):
    rows = h2d.shape[0]
    const = lambda *idx: (lambda i: idx)
    in_specs = [
        _resident((rows, D_MODEL), const(0, 0)),
        _resident((rows, MIX_W), const(0, 0)),
        _resident((None, rows, PLE_DIM), const(layer, 0, 0)),
        _resident((MIX_W, D_MODEL), const(0, 0)),
        _resident((None, 1, D_MODEL), const(layer, 0, 0)),
        _resident((D_MODEL, D_MODEL), const(0, 0)),
        _resident((None, PLE_DIM, D_MODEL), const(layer, 0, 0)),
        _resident((1, D_MODEL), const(0, 0)),
    ]
    return pl.pallas_call(
        functools.partial(_output_sample_kernel, final_norm=(layer == DEPTH - 1)),
        grid=(1,),
        in_specs=in_specs,
        out_specs=pl.BlockSpec((rows, D_MODEL), const(0, 0)),
        out_shape=jax.ShapeDtypeStruct((rows, D_MODEL), F32),
        compiler_params=pltpu.CompilerParams(
            dimension_semantics=("arbitrary",),
            vmem_limit_bytes=VMEM_LIMIT_BYTES),
        name=f"output_sample_l{layer}",
    )(h2d, mix2d, p2d, wout_bf16, w["g_ple"], wpg_bf16, w["w_ple"], w["g_final"])


def _prompt_trunk(x, p, w):
    B, L, _ = x.shape
    tables = _retention_tables(RET_CHUNK, RET_CHUNK)
    rope = _rope_tables(jnp.arange(L))
    p2d = p.reshape(DEPTH, B * L, PLE_DIM)
    h = x
    hists, states, w_bf16 = [], [], []
    for layer in range(DEPTH):
        mix, hist, state, win_b = _mixer_prompt(h, layer, 0, tables, rope, w)
        h, wout_b, wpg_b = _output_prompt_call(h.reshape(B * L, D_MODEL),
                                               mix.reshape(B * L, MIX_W), p2d, layer, w)
        h = h.reshape(B, L, D_MODEL)
        hists.append(hist)
        states.append(state)
        w_bf16.append((win_b, wout_b, wpg_b))
    return h, jnp.stack(hists, axis=0), jnp.stack(states, axis=0), w_bf16


def _sample_trunk(x, p, hist0, s0, offset, w, w_bf16):
    B, L, _ = x.shape
    rows = B * L
    tables = _retention_tables(rows, L)
    rope = _rope_tables(offset + jnp.arange(rows) % L)
    p2d = p.reshape(DEPTH, rows, PLE_DIM)
    hist0 = jnp.pad(hist0, ((0, 0), (0, 0), (HIST_ROWS - POOL_HIST, 0), (0, 0)))
    h = x.reshape(rows, D_MODEL)
    hists, states = [], []
    for layer in range(DEPTH):
        win_b, wout_b, wpg_b = w_bf16[layer]
        mix, hist, state = _mixer_sample(h, hist0, s0, layer, offset, L, tables, rope, w, win_b)
        h = _output_sample_call(h, mix, p2d, layer, w, wout_b, wpg_b)
        hists.append(hist)
        states.append(state)
    return h.reshape(B, L, D_MODEL), jnp.stack(hists, axis=0), jnp.stack(states, axis=0)


def kernel(x_prompt, x_sample, p_prompt, p_sample, state_pool, state_ret, g_mix, w_in, w_pool,
           pool_scale, g_gn, w_out, g_ple, w_pg, w_ple, g_final):
    past_len = 1024
    row = lambda a: a.reshape(a.shape[0], 1, a.shape[1])
    w = {
        "g_mix": row(g_mix), "w_in": w_in, "w_pool": w_pool.astype(BF16),
        "pool_scale": row(pool_scale), "g_gn": row(g_gn), "w_out": w_out,
        "g_ple": row(g_ple), "w_pg": w_pg, "w_ple": w_ple.astype(BF16),
        "g_final": g_final.reshape(1, D_MODEL),
    }
    y_prompt, pool_prompt, ret_prompt, w_bf16 = _prompt_trunk(x_prompt, p_prompt, w)
    y_sample, pool_sample, ret_sample = _sample_trunk(
        x_sample, p_sample, state_pool, state_ret, past_len, w, w_bf16)
    return (y_prompt, y_sample, pool_prompt, ret_prompt, pool_sample, ret_sample)
```

```python
import functools

import jax
import jax.numpy as jnp
import numpy as np
from jax import lax
from jax.experimental import pallas as pl
from jax.experimental.pallas import tpu as pltpu

D_MODEL = 2048
DEPTH = 4
PLE_DIM = 256
POOL_W = 1024
POOL_WINDOWS = (2, 4, 8, 16)
POOL_GW = POOL_W // len(POOL_WINDOWS)
POOL_HIST = max(POOL_WINDOWS) - 1
HIST_ROWS = POOL_HIST + 1
RET_HEADS = 4
RET_DK = 256
RET_DV = 256
RET_W = RET_HEADS * RET_DV
MIX_W = POOL_W + RET_W
IN_W = 2 * POOL_W + 2 * RET_HEADS * RET_DK + 2 * RET_W
ROPE_BASE = 10000.0
EPS = 1e-6
GN_EPS = 1e-5
HALF = RET_DK // 2

RET_CHUNK = 256
ROW_TILE = 512
OUT_ROW_TILE = 512
SEQS_PER_STEP = 2
VMEM_LIMIT_BYTES = 56 * 1024 * 1024

F32 = jnp.float32
BF16 = jnp.bfloat16


def _resident(block_shape, index_map):
    return pl.BlockSpec(block_shape, index_map, pipeline_mode=pl.Buffered(1))


def _rmsnorm(x, g):
    ms = jnp.mean(x * x, axis=-1, keepdims=True)
    return x * lax.rsqrt(ms + EPS) * g


def _silu(x):
    return x * jax.nn.sigmoid(x)


def _dot(a, b):
    return jnp.dot(a, b, preferred_element_type=F32)


def _dot_nt(a, b):
    return lax.dot_general(a, b, (((1,), (1,)), ((), ())), preferred_element_type=F32)


def _dot_tn(a, b):
    return lax.dot_general(a, b, (((0,), (0,)), ((), ())), preferred_element_type=F32)


def _rotate(x, cos, sin):
    x1, x2 = x[:, :HALF], x[:, HALF:]
    return jnp.concatenate([x1 * cos - x2 * sin, x1 * sin + x2 * cos], axis=-1)


def _both_halves(ref, hd):
    return jnp.concatenate([ref[hd], ref[hd]], axis=-1)


def _pool_inputs(ubuf_ref, pos, seq_rows):
    n_seq = ubuf_ref.shape[0]
    rows = n_seq * seq_rows
    pooled = []
    for g, w in enumerate(POOL_WINDOWS):
        cols = slice(g * POOL_GW, (g + 1) * POOL_GW)
        u_g = ubuf_ref[:, HIST_ROWS:HIST_ROWS + seq_rows, cols]
        acc = u_g
        for j in range(1, w):
            acc = acc + ubuf_ref[:, HIST_ROWS - j:HIST_ROWS - j + seq_rows, cols]
        acc = acc.reshape(rows, POOL_GW)
        u_g = u_g.reshape(rows, POOL_GW)
        cnt = jnp.minimum(pos + 1, w).astype(F32)
        pooled.append((acc / cnt - u_g).astype(BF16))
    return pooled


def _pool_outputs(pooled, gp, wpool_ref, pscale_ref, mix_ref):
    for g in range(len(POOL_WINDOWS)):
        cols = slice(g * POOL_GW, (g + 1) * POOL_GW)
        y = _dot(pooled[g], wpool_ref[g])
        y = y * pscale_ref[:, cols] * _silu(gp[:, cols])
        mix_ref[:, cols] = y.astype(mix_ref.dtype)


def _group_norm_gate(o, ggn, gate):
    mu = jnp.mean(o, axis=-1, keepdims=True)
    d = o - mu
    var = jnp.mean(d * d, axis=-1, keepdims=True)
    return d * lax.rsqrt(var + GN_EPS) * ggn * gate


class _SideConvert:
    def __init__(self, src_hbm, layer, dst_hbm, in_ref, out_ref, sems, step, n_steps):
        self.src, self.layer, self.dst = src_hbm, layer, dst_hbm
        self.in_ref, self.out_ref, self.sems = in_ref, out_ref, sems
        self.step, self.n_steps = step, n_steps
        self.rows = in_ref.shape[0]

    def _load(self, j):
        r0 = pl.multiple_of(j * self.rows, self.rows)
        return pltpu.make_async_copy(self.src.at[self.layer, pl.ds(r0, self.rows), :],
                                     self.in_ref, self.sems.at[0])

    def _store(self, j):
        r0 = pl.multiple_of(j * self.rows, self.rows)
        return pltpu.make_async_copy(self.out_ref, self.dst.at[pl.ds(r0, self.rows), :],
                                     self.sems.at[1])

    def head(self):
        @pl.when(self.step == 0)
        def _():
            self._load(0).start()

        self._load(self.step).wait()

        @pl.when(self.step > 0)
        def _():
            self._store(self.step - 1).wait()

    def body(self):
        self.out_ref[...] = self.in_ref[...].astype(self.out_ref.dtype)
        self._store(self.step).start()
        self._load(jnp.minimum(self.step + 1, self.n_steps - 1)).start()

    def tail(self):
        @pl.when(self.step == self.n_steps - 1)
        def _():
            self._store(self.step).wait()
            self._load(self.step).wait()


def _mixer_prompt_kernel(h_ref, cos_ref, sin_ref, dmat_ref, qdec_ref, kdec_ref, cdec_ref,
                         gmix_ref, win_ref, wpool_ref, pscale_ref, ggn_ref, wout_hbm, wpg_hbm,
                         mix_ref, hist_ref, s_ref, wout_bf16_hbm, wpg_bf16_hbm,
                         ubuf_ref, wout_in, wout_out, wpg_in, wpg_out, side_sems,
                         *, offset, layer):
    T = h_ref.shape[0]
    C = dmat_ref.shape[1]
    t = pl.program_id(1)
    step = pl.program_id(0) * pl.num_programs(1) + t
    n_steps = pl.num_programs(0) * pl.num_programs(1)
    sides = [
        _SideConvert(wout_hbm, layer, wout_bf16_hbm, wout_in, wout_out, side_sems.at[0],
                     step, n_steps),
        _SideConvert(wpg_hbm, layer, wpg_bf16_hbm, wpg_in, wpg_out, side_sems.at[1],
                     step, n_steps),
    ]
    for side in sides:
        side.head()

    @pl.when(t == 0)
    def _():
        ubuf_ref[:, :HIST_ROWS, :] = jnp.zeros((1, HIST_ROWS, POOL_W), F32)
        s_ref[...] = jnp.zeros(s_ref.shape, F32)

    for side in sides:
        side.body()

    for c in range(T // C):
        rows = slice(c * C, (c + 1) * C)
        hn = _rmsnorm(h_ref[rows, :], gmix_ref[...]).astype(BF16)

        def proj(j, hn=hn):
            return _dot(hn, win_ref[:, j * POOL_W:(j + 1) * POOL_W])

        ubuf_ref[0, HIST_ROWS + c * C:HIST_ROWS + (c + 1) * C, :] = proj(0)
        gp = proj(1)
        pos = offset + t * T + c * C + lax.broadcasted_iota(jnp.int32, (C, POOL_GW), 0)
        pooled = _pool_inputs(ubuf_ref.at[:, c * C:HIST_ROWS + (c + 1) * C, :], pos, C)
        q, k, v, gr = proj(2), proj(3), proj(4).astype(BF16), proj(5)
        _pool_outputs(pooled, gp, wpool_ref, pscale_ref, mix_ref.at[rows, :])

        cos, sin = cos_ref[rows, :], sin_ref[rows, :]
        head_cols = [slice(hd * RET_DK, (hd + 1) * RET_DK) for hd in range(RET_HEADS)]
        scores, cross = [], []
        for hd, cols in enumerate(head_cols):
            qr = _rotate(q[:, cols], cos, sin).astype(BF16)
            kr = _rotate(k[:, cols], cos, sin) * (RET_DK ** -0.5)
            kd = (kr * _both_halves(kdec_ref, hd)).astype(BF16)
            scores.append(_dot_nt(qr, kr.astype(BF16)))
            s_old = s_ref[hd]
            cross.append(_dot(qr, s_old.astype(BF16)) * _both_halves(qdec_ref, hd))
            s_ref[hd] = s_old * cdec_ref[hd] + _dot_tn(kd, v[:, cols])
        for hd, cols in enumerate(head_cols):
            p = (scores[hd] * dmat_ref[hd]).astype(BF16)
            o = _dot(p, v[:, cols]) + cross[hd]
            on = _group_norm_gate(o, ggn_ref[:, cols], _silu(gr[:, cols]))
            mix_ref[rows, POOL_W + hd * RET_DV:POOL_W + (hd + 1) * RET_DV] = on.astype(mix_ref.dtype)

    @pl.when(t == pl.num_programs(1) - 1)
    def _():
        hist_ref[...] = ubuf_ref[0, T + 1:T + HIST_ROWS, :]

    ubuf_ref[0, :HIST_ROWS, :] = ubuf_ref[0, T:T + HIST_ROWS, :]

    for side in sides:
        side.tail()


def _mixer_sample_kernel(h_ref, cos_ref, sin_ref, dmat_ref, qdec_ref, kdec_ref, cdec_ref,
                         hist0_ref, s0_ref, gmix_ref, win_ref, wpool_ref, pscale_ref, ggn_ref,
                         mix_ref, hist_ref, s_ref,
                         ubuf_ref, qr_ref, kdt_ref, v_ref, oin_ref, ox_ref, gate_ref,
                         *, offset, seq_rows):
    rows = h_ref.shape[0]
    s = pl.program_id(0)

    @pl.when(s == 0)
    def _():
        n_seq = rows // seq_rows
        hn = _rmsnorm(h_ref[...], gmix_ref[...]).astype(BF16)

        def proj(j):
            return _dot(hn, win_ref[:, j * POOL_W:(j + 1) * POOL_W])

        ubuf_ref[:, :HIST_ROWS, :] = hist0_ref[...]
        ubuf_ref[:, HIST_ROWS:, :] = proj(0).reshape(n_seq, seq_rows, POOL_W)
        gp = proj(1)
        row = lax.broadcasted_iota(jnp.int32, (rows, POOL_GW), 0)
        pos = offset + lax.rem(row, seq_rows)
        pooled = _pool_inputs(ubuf_ref, pos, seq_rows)
        hist_ref[...] = ubuf_ref[:, seq_rows + 1:seq_rows + HIST_ROWS, :]
        q, k = proj(2), proj(3)
        _pool_outputs(pooled, gp, wpool_ref, pscale_ref, mix_ref)

        v_ref[...] = proj(4).astype(BF16)
        gate_ref[...] = _silu(proj(5))
        cos, sin = cos_ref[...], sin_ref[...]
        for hd in range(RET_HEADS):
            cols = slice(hd * RET_DK, (hd + 1) * RET_DK)
            qr = _rotate(q[:, cols], cos, sin).astype(BF16)
            kr = _rotate(k[:, cols], cos, sin) * (RET_DK ** -0.5)
            qr_ref[:, cols] = qr
            kdt_ref[hd] = (kr * _both_halves(kdec_ref, hd)).T.astype(BF16)
            p = (_dot_nt(qr, kr.astype(BF16)) * dmat_ref[hd]).astype(BF16)
            oin_ref[:, cols] = _dot(p, v_ref[:, cols])

    row_id = lax.broadcasted_iota(jnp.int32, (rows, RET_DV), 0)
    for i in range(s0_ref.shape[0]):
        r0 = pl.multiple_of((s * s0_ref.shape[0] + i) * seq_rows, seq_rows)
        in_seq = (row_id >= r0) & (row_id < r0 + seq_rows)
        for hd in range(RET_HEADS):
            cols = slice(hd * RET_DK, (hd + 1) * RET_DK)
            s_old = s0_ref[i, hd]
            ox_ref[pl.ds(r0, seq_rows), cols] = _dot(qr_ref[pl.ds(r0, seq_rows), cols],
                                                     s_old.astype(BF16))
            vh = v_ref[:, cols]
            v_seq = jnp.where(in_seq, vh, jnp.zeros_like(vh))
            s_ref[i, hd] = s_old * cdec_ref[hd] + _dot(kdt_ref[hd], v_seq)

    @pl.when(s == pl.num_programs(0) - 1)
    def _():
        for hd in range(RET_HEADS):
            cols = slice(hd * RET_DK, (hd + 1) * RET_DK)
            o = oin_ref[:, cols] + ox_ref[:, cols] * _both_halves(qdec_ref, hd)
            on = _group_norm_gate(o, ggn_ref[:, cols], gate_ref[:, cols])
            mix_ref[:, POOL_W + hd * RET_DV:POOL_W + (hd + 1) * RET_DV] = on.astype(mix_ref.dtype)


def _output_body(h_ref, mix_ref, p_ref, wout_ref, gple_ref, wpg_ref, wple_ref, gfin_ref,
                 out_ref, final_norm):
    h = h_ref[...] + _dot(mix_ref[...], wout_ref[...])
    hn = _rmsnorm(h, gple_ref[...]).astype(BF16)
    gate = jax.nn.sigmoid(_dot(hn, wpg_ref[...]))
    h = h + gate * _dot(p_ref[...].astype(BF16), wple_ref[...])
    if final_norm:
        h = _rmsnorm(h, gfin_ref[...])
    out_ref[...] = h


def _output_kernel(h_ref, mix_ref, p_ref, wout_ref, gple_ref, wpg_ref, wple_ref, gfin_ref,
                   out_ref, *, final_norm):
    _output_body(h_ref, mix_ref, p_ref, wout_ref, gple_ref, wpg_ref, wple_ref, gfin_ref,
                 out_ref, final_norm)


def _output_convert_kernel(h_ref, mix_ref, p_ref, wout_ref, gple_ref, wpg_ref, wple_ref, gfin_ref,
                           win_next_hbm, out_ref, win_next_bf16_hbm,
                           win_in, win_out, side_sems, *, final_norm, next_layer):
    side = _SideConvert(win_next_hbm, next_layer, win_next_bf16_hbm, win_in, win_out, side_sems,
                        pl.program_id(0), pl.num_programs(0))
    side.head()
    side.body()
    _output_body(h_ref, mix_ref, p_ref, wout_ref, gple_ref, wpg_ref, wple_ref, gfin_ref,
                 out_ref, final_norm)
    side.tail()


def _retention_tables(rows, seq_rows):
    lg = np.log(1.0 - 2.0 ** (-5.0 - np.arange(RET_HEADS, dtype=np.float64)))
    n = np.arange(rows)
    r = (n % seq_rows).astype(np.float64)
    same = (n // seq_rows)[:, None] == (n // seq_rows)[None, :]
    diff = r[:, None] - r[None, :]
    keep = same & (diff >= 0)
    dmat = np.where(keep[None], np.exp(np.maximum(diff, 0.0)[None] * lg[:, None, None]), 0.0)
    qdec = np.exp((r[None, :] + 1.0) * lg[:, None])
    kdec = np.exp((seq_rows - 1.0 - r[None, :]) * lg[:, None])
    cdec = np.exp(seq_rows * lg)
    bcast = lambda a: np.broadcast_to(a[:, :, None], (RET_HEADS, rows, HALF))
    return tuple(jnp.asarray(a, F32) for a in (dmat, bcast(qdec), bcast(kdec), cdec))


def _rope_tables(pos):
    inv = 1.0 / (ROPE_BASE ** (jnp.arange(HALF, dtype=F32) / HALF))
    ang = pos.astype(F32)[:, None] * inv[None, :]
    return jnp.cos(ang), jnp.sin(ang)


def _mixer_prompt(h, layer, offset, tables, rope, w, win_bf16):
    B, L, _ = h.shape
    T = ROW_TILE
    n_steps = B * (L // T)
    dmat, qdec, kdec, cdec = tables
    cos, sin = rope
    const = lambda *idx: (lambda b, t: idx)
    in_specs = [
        pl.BlockSpec((None, T, D_MODEL), lambda b, t: (b, t, 0)),
        pl.BlockSpec((T, HALF), lambda b, t: (t, 0)),
        pl.BlockSpec((T, HALF), lambda b, t: (t, 0)),
        _resident((RET_HEADS, RET_CHUNK, RET_CHUNK), const(0, 0, 0)),
        _resident((RET_HEADS, RET_CHUNK, HALF), const(0, 0, 0)),
        _resident((RET_HEADS, RET_CHUNK, HALF), const(0, 0, 0)),
        pl.BlockSpec(memory_space=pltpu.SMEM),
        _resident((None, 1, D_MODEL), const(layer, 0, 0)),
        _resident((D_MODEL, IN_W), const(0, 0)),
        _resident((None, len(POOL_WINDOWS), POOL_GW, POOL_GW), const(layer, 0, 0, 0)),
        _resident((None, 1, POOL_W), const(layer, 0, 0)),
        _resident((None, 1, RET_W), const(layer, 0, 0)),
        pl.BlockSpec(memory_space=pl.ANY),
        pl.BlockSpec(memory_space=pl.ANY),
    ]
    out_specs = [
        pl.BlockSpec((None, T, MIX_W), lambda b, t: (b, t, 0)),
        pl.BlockSpec((None, POOL_HIST, POOL_W), lambda b, t: (b, 0, 0)),
        pl.BlockSpec((None, RET_HEADS, RET_DK, RET_DV), lambda b, t: (b, 0, 0, 0)),
        pl.BlockSpec(memory_space=pl.ANY),
        pl.BlockSpec(memory_space=pl.ANY),
    ]
    out_shape = [
        jax.ShapeDtypeStruct((B, L, MIX_W), BF16),
        jax.ShapeDtypeStruct((B, POOL_HIST, POOL_W), F32),
        jax.ShapeDtypeStruct((B, RET_HEADS, RET_DK, RET_DV), F32),
        jax.ShapeDtypeStruct((MIX_W, D_MODEL), BF16),
        jax.ShapeDtypeStruct((D_MODEL, D_MODEL), BF16),
    ]
    scratch_shapes = [
        pltpu.VMEM((1, HIST_ROWS + T, POOL_W), F32),
        pltpu.VMEM((MIX_W // n_steps, D_MODEL), F32),
        pltpu.VMEM((MIX_W // n_steps, D_MODEL), BF16),
        pltpu.VMEM((D_MODEL // n_steps, D_MODEL), F32),
        pltpu.VMEM((D_MODEL // n_steps, D_MODEL), BF16),
        pltpu.SemaphoreType.DMA((2, 2)),
    ]
    return pl.pallas_call(
        functools.partial(_mixer_prompt_kernel, offset=offset, layer=layer),
        grid=(B, L // T),
        in_specs=in_specs,
        out_specs=out_specs,
        out_shape=out_shape,
        scratch_shapes=scratch_shapes,
        compiler_params=pltpu.CompilerParams(
            dimension_semantics=("arbitrary", "arbitrary"),
            vmem_limit_bytes=VMEM_LIMIT_BYTES),
        name=f"mixer_prompt_l{layer}",
    )(h, cos, sin, dmat, qdec, kdec, cdec, w["g_mix"], win_bf16, w["w_pool"],
      w["pool_scale"], w["g_gn"], w["w_out"], w["w_pg"])


def _mixer_sample(h2d, hist0, s0, layer, offset, seq_rows, tables, rope, w, win_bf16):
    rows = h2d.shape[0]
    n_seq = rows // seq_rows
    dmat, qdec, kdec, cdec = tables
    cos, sin = rope
    const = lambda *idx: (lambda s: idx)
    in_specs = [
        _resident((rows, D_MODEL), const(0, 0)),
        _resident((rows, HALF), const(0, 0)),
        _resident((rows, HALF), const(0, 0)),
        _resident((RET_HEADS, rows, rows), const(0, 0, 0)),
        _resident((RET_HEADS, rows, HALF), const(0, 0, 0)),
        _resident((RET_HEADS, rows, HALF), const(0, 0, 0)),
        pl.BlockSpec(memory_space=pltpu.SMEM),
        _resident((None, n_seq, HIST_ROWS, POOL_W), const(layer, 0, 0, 0)),
        pl.BlockSpec((None, SEQS_PER_STEP, RET_HEADS, RET_DK, RET_DV),
                     lambda s: (layer, s, 0, 0, 0)),
        _resident((None, 1, D_MODEL), const(layer, 0, 0)),
        _resident((D_MODEL, IN_W), const(0, 0)),
        _resident((None, len(POOL_WINDOWS), POOL_GW, POOL_GW), const(layer, 0, 0, 0)),
        _resident((None, 1, POOL_W), const(layer, 0, 0)),
        _resident((None, 1, RET_W), const(layer, 0, 0)),
    ]
    out_specs = [
        pl.BlockSpec((rows, MIX_W), const(0, 0)),
        pl.BlockSpec((n_seq, POOL_HIST, POOL_W), const(0, 0, 0)),
        pl.BlockSpec((SEQS_PER_STEP, RET_HEADS, RET_DK, RET_DV), lambda s: (s, 0, 0, 0)),
    ]
    out_shape = [
        jax.ShapeDtypeStruct((rows, MIX_W), BF16),
        jax.ShapeDtypeStruct((n_seq, POOL_HIST, POOL_W), F32),
        jax.ShapeDtypeStruct((n_seq, RET_HEADS, RET_DK, RET_DV), F32),
    ]
    scratch_shapes = [
        pltpu.VMEM((n_seq, HIST_ROWS + seq_rows, POOL_W), F32),
        pltpu.VMEM((rows, RET_W), BF16),
        pltpu.VMEM((RET_HEADS, RET_DK, rows), BF16),
        pltpu.VMEM((rows, RET_W), BF16),
        pltpu.VMEM((rows, RET_W), F32),
        pltpu.VMEM((rows, RET_W), F32),
        pltpu.VMEM((rows, RET_W), F32),
    ]
    return pl.pallas_call(
        functools.partial(_mixer_sample_kernel, offset=offset, seq_rows=seq_rows),
        grid=(n_seq // SEQS_PER_STEP,),
        in_specs=in_specs,
        out_specs=out_specs,
        out_shape=out_shape,
        scratch_shapes=scratch_shapes,
        compiler_params=pltpu.CompilerParams(
            dimension_semantics=("arbitrary",),
            vmem_limit_bytes=VMEM_LIMIT_BYTES),
        name=f"mixer_sample_l{layer}",
    )(h2d, cos, sin, dmat, qdec, kdec, cdec, hist0, s0, w["g_mix"], win_bf16, w["w_pool"],
      w["pool_scale"], w["g_gn"])


def _output_call(h2d, mix2d, p2d, layer, w, wout_bf16, wpg_bf16, tag, convert_next):
    rows = h2d.shape[0]
    T = min(OUT_ROW_TILE, rows)
    n_steps = rows // T
    final_norm = layer == DEPTH - 1
    const = lambda *idx: (lambda i: idx)
    in_specs = [
        pl.BlockSpec((T, D_MODEL), lambda i: (i, 0)),
        pl.BlockSpec((T, MIX_W), lambda i: (i, 0)),
        pl.BlockSpec((None, T, PLE_DIM), lambda i: (layer, i, 0)),
        _resident((MIX_W, D_MODEL), const(0, 0)),
        _resident((None, 1, D_MODEL), const(layer, 0, 0)),
        _resident((D_MODEL, D_MODEL), const(0, 0)),
        _resident((None, PLE_DIM, D_MODEL), const(layer, 0, 0)),
        _resident((1, D_MODEL), const(0, 0)),
    ]
    out_specs = [pl.BlockSpec((T, D_MODEL), lambda i: (i, 0))]
    out_shape = [jax.ShapeDtypeStruct((rows, D_MODEL), F32)]
    operands = [h2d, mix2d, p2d, wout_bf16, w["g_ple"], wpg_bf16, w["w_ple"], w["g_final"]]
    scratch_shapes = []
    if convert_next:
        body = functools.partial(_output_convert_kernel, final_norm=final_norm,
                                 next_layer=layer + 1)
        in_specs.append(pl.BlockSpec(memory_space=pl.ANY))
        operands.append(w["w_in"])
        out_specs.append(pl.BlockSpec(memory_space=pl.ANY))
        out_shape.append(jax.ShapeDtypeStruct((D_MODEL, IN_W), BF16))
        scratch_shapes = [
            pltpu.VMEM((D_MODEL // n_steps, IN_W), F32),
            pltpu.VMEM((D_MODEL // n_steps, IN_W), BF16),
            pltpu.SemaphoreType.DMA((2,)),
        ]
    else:
        body = functools.partial(_output_kernel, final_norm=final_norm)
    return pl.pallas_call(
        body,
        grid=(n_steps,),
        in_specs=in_specs,
        out_specs=out_specs,
        out_shape=out_shape,
        scratch_shapes=scratch_shapes,
        compiler_params=pltpu.CompilerParams(
            dimension_semantics=("arbitrary",),
            vmem_limit_bytes=VMEM_LIMIT_BYTES),
        name=f"output_{tag}_l{layer}",
    )(*operands)


def _prompt_trunk(x, p, w):
    B, L, _ = x.shape
    tables = _retention_tables(RET_CHUNK, RET_CHUNK)
    rope = _rope_tables(jnp.arange(L))
    p2d = p.reshape(DEPTH, B * L, PLE_DIM)
    h = x
    hists, states, w_bf16 = [], [], []
    win_b = w["w_in"][0].astype(BF16)
    for layer in range(DEPTH):
        mix, hist, state, wout_b, wpg_b = _mixer_prompt(h, layer, 0, tables, rope, w, win_b)
        w_bf16.append((win_b, wout_b, wpg_b))
        outs = _output_call(h.reshape(B * L, D_MODEL), mix.reshape(B * L, MIX_W), p2d, layer, w,
                            wout_b, wpg_b, "prompt", convert_next=layer + 1 < DEPTH)
        h = outs[0].reshape(B, L, D_MODEL)
        if layer + 1 < DEPTH:
            win_b = outs[1]
        hists.append(hist)
        states.append(state)
    return h, jnp.stack(hists, axis=0), jnp.stack(states, axis=0), w_bf16


def _sample_trunk(x, p, hist0, s0, offset, w, w_bf16):
    B, L, _ = x.shape
    rows = B * L
    tables = _retention_tables(rows, L)
    rope = _rope_tables(offset + jnp.arange(rows) % L)
    p2d = p.reshape(DEPTH, rows, PLE_DIM)
    hist0 = jnp.pad(hist0, ((0, 0), (0, 0), (HIST_ROWS - POOL_HIST, 0), (0, 0)))
    h = x.reshape(rows, D_MODEL)
    hists, states = [], []
    for layer in range(DEPTH):
        win_b, wout_b, wpg_b = w_bf16[layer]
        mix, hist, state = _mixer_sample(h, hist0, s0, layer, offset, L, tables, rope, w, win_b)
        h = _output_call(h, mix, p2d, layer, w, wout_b, wpg_b, "sample", convert_next=False)[0]
        hists.append(hist)
        states.append(state)
    return h.reshape(B, L, D_MODEL), jnp.stack(hists, axis=0), jnp.stack(states, axis=0)


def kernel(x_prompt, x_sample, p_prompt, p_sample, state_pool, state_ret, g_mix, w_in, w_pool,
           pool_scale, g_gn, w_out, g_ple, w_pg, w_ple, g_final):
    past_len = 1024
    row = lambda a: a.reshape(a.shape[0], 1, a.shape[1])
    w = {
        "g_mix": row(g_mix), "w_in": w_in, "w_pool": w_pool.astype(BF16),
        "pool_scale": row(pool_scale), "g_gn": row(g_gn), "w_out": w_out,
        "g_ple": row(g_ple), "w_pg": w_pg, "w_ple": w_ple.astype(BF16),
        "g_final": g_final.reshape(1, D_MODEL),
    }
    y_prompt, pool_prompt, ret_prompt, w_bf16 = _prompt_trunk(x_prompt, p_prompt, w)
    y_sample, pool_sample, ret_sample = _sample_trunk(
        x_sample, p_sample, state_pool, state_ret, past_len, w, w_bf16)
    return (y_prompt, y_sample, pool_prompt, ret_prompt, pool_sample, ret_sample)
```

```python
import functools

import jax
import jax.numpy as jnp
import numpy as np
from jax import lax
from jax.experimental import pallas as pl
from jax.experimental.pallas import tpu as pltpu

D_MODEL = 2048
DEPTH = 4
PLE_DIM = 256
POOL_W = 1024
POOL_WINDOWS = (2, 4, 8, 16)
POOL_GW = POOL_W // len(POOL_WINDOWS)
POOL_HIST = max(POOL_WINDOWS) - 1
HIST_ROWS = POOL_HIST + 1
RET_HEADS = 4
RET_DK = 256
RET_DV = 256
RET_W = RET_HEADS * RET_DV
MIX_W = POOL_W + RET_W
IN_W = 2 * POOL_W + 2 * RET_HEADS * RET_DK + 2 * RET_W
ROPE_BASE = 10000.0
EPS = 1e-6
GN_EPS = 1e-5
HALF = RET_DK // 2

RET_CHUNK = 256
ROW_TILE = 512
OUT_ROW_TILE = 512
SEQS_PER_STEP = 2
VMEM_LIMIT_BYTES = 56 * 1024 * 1024

F32 = jnp.float32
BF16 = jnp.bfloat16


def _resident(block_shape, index_map):
    return pl.BlockSpec(block_shape, index_map, pipeline_mode=pl.Buffered(1))


def _rmsnorm(x, g):
    ms = jnp.mean(x * x, axis=-1, keepdims=True)
    return x * lax.rsqrt(ms + EPS) * g


def _silu(x):
    return x * jax.nn.sigmoid(x)


def _dot(a, b):
    return jnp.dot(a, b, preferred_element_type=F32)


def _dot_nt(a, b):
    return lax.dot_general(a, b, (((1,), (1,)), ((), ())), preferred_element_type=F32)


def _dot_tn(a, b):
    return lax.dot_general(a, b, (((0,), (0,)), ((), ())), preferred_element_type=F32)


def _rotate(x, cos, sin):
    x1, x2 = x[:, :HALF], x[:, HALF:]
    return jnp.concatenate([x1 * cos - x2 * sin, x1 * sin + x2 * cos], axis=-1)


def _both_halves(ref, hd):
    return jnp.concatenate([ref[hd], ref[hd]], axis=-1)


def _pool_inputs(ubuf_ref, pos, seq_rows):
    n_seq = ubuf_ref.shape[0]
    rows = n_seq * seq_rows
    pooled = []
    for g, w in enumerate(POOL_WINDOWS):
        cols = slice(g * POOL_GW, (g + 1) * POOL_GW)
        u_g = ubuf_ref[:, HIST_ROWS:HIST_ROWS + seq_rows, cols]
        acc = u_g
        for j in range(1, w):
            acc = acc + ubuf_ref[:, HIST_ROWS - j:HIST_ROWS - j + seq_rows, cols]
        acc = acc.reshape(rows, POOL_GW)
        u_g = u_g.reshape(rows, POOL_GW)
        cnt = jnp.minimum(pos + 1, w).astype(F32)
        pooled.append((acc / cnt - u_g).astype(BF16))
    return pooled


def _pool_outputs(pooled, gp, wpool_ref, pscale_ref, mix_ref):
    for g in range(len(POOL_WINDOWS)):
        cols = slice(g * POOL_GW, (g + 1) * POOL_GW)
        y = _dot(pooled[g], wpool_ref[g])
        y = y * pscale_ref[:, cols] * _silu(gp[:, cols])
        mix_ref[:, cols] = y.astype(mix_ref.dtype)


def _group_norm_gate(o, ggn, gate):
    mu = jnp.mean(o, axis=-1, keepdims=True)
    d = o - mu
    var = jnp.mean(d * d, axis=-1, keepdims=True)
    return d * lax.rsqrt(var + GN_EPS) * ggn * gate


class _SideConvert:
    def __init__(self, src_hbm, layer, dst_hbm, in_ref, out_ref, sems, step, n_steps):
        self.src, self.layer, self.dst = src_hbm, layer, dst_hbm
        self.in_ref, self.out_ref, self.sems = in_ref, out_ref, sems
        self.step, self.n_steps = step, n_steps
        self.rows = in_ref.shape[0]

    def _load(self, j):
        r0 = pl.multiple_of(j * self.rows, self.rows)
        return pltpu.make_async_copy(self.src.at[self.layer, pl.ds(r0, self.rows), :],
                                     self.in_ref, self.sems.at[0])

    def _store(self, j):
        r0 = pl.multiple_of(j * self.rows, self.rows)
        return pltpu.make_async_copy(self.out_ref, self.dst.at[pl.ds(r0, self.rows), :],
                                     self.sems.at[1])

    def head(self):
        @pl.when(self.step == 0)
        def _():
            self._load(0).start()

        self._load(self.step).wait()

        @pl.when(self.step > 0)
        def _():
            self._store(self.step - 1).wait()

        @pl.when(self.step < self.n_steps)
        def _():
            self.out_ref[...] = self.in_ref[...].astype(self.out_ref.dtype)
            self._store(self.step).start()
            self._load(jnp.minimum(self.step + 1, self.n_steps - 1)).start()

    def tail(self):
        @pl.when(self.step == self.n_steps - 1)
        def _():
            self._store(self.step).wait()
            self._load(self.step).wait()


def _mixer_prompt_kernel(h_ref, cos_ref, sin_ref, dmat_ref, qdec_ref, kdec_ref, cdec_ref,
                         gmix_ref, win_ref, wpool_ref, pscale_ref, ggn_ref, wout_hbm, wpg_hbm,
                         mix_ref, hist_ref, s_ref, wout_bf16_hbm, wpg_bf16_hbm,
                         ubuf_ref, wout_in, wout_out, wout_sems, wpg_in, wpg_out, wpg_sems,
                         *, offset, layer):
    T = h_ref.shape[0]
    C = dmat_ref.shape[1]
    t = pl.program_id(1)
    step = pl.program_id(0) * pl.num_programs(1) + t
    n_steps = pl.num_programs(0) * pl.num_programs(1)
    sides = [
        _SideConvert(wout_hbm, layer, wout_bf16_hbm, wout_in, wout_out, wout_sems, step, n_steps),
        _SideConvert(wpg_hbm, layer, wpg_bf16_hbm, wpg_in, wpg_out, wpg_sems, step, n_steps),
    ]
    for side in sides:
        side.head()

    @pl.when(t == 0)
    def _():
        ubuf_ref[:, :HIST_ROWS, :] = jnp.zeros((1, HIST_ROWS, POOL_W), F32)
        s_ref[...] = jnp.zeros(s_ref.shape, F32)

    for c in range(T // C):
        rows = slice(c * C, (c + 1) * C)
        hn = _rmsnorm(h_ref[rows, :], gmix_ref[...]).astype(BF16)

        def proj(j, hn=hn):
            return _dot(hn, win_ref[:, j * POOL_W:(j + 1) * POOL_W])

        ubuf_ref[0, HIST_ROWS + c * C:HIST_ROWS + (c + 1) * C, :] = proj(0)
        gp = proj(1)
        pos = offset + t * T + c * C + lax.broadcasted_iota(jnp.int32, (C, POOL_GW), 0)
        pooled = _pool_inputs(ubuf_ref.at[:, c * C:HIST_ROWS + (c + 1) * C, :], pos, C)
        q, k, v, gr = proj(2), proj(3), proj(4).astype(BF16), proj(5)
        _pool_outputs(pooled, gp, wpool_ref, pscale_ref, mix_ref.at[rows, :])

        cos, sin = cos_ref[rows, :], sin_ref[rows, :]
        head_cols = [slice(hd * RET_DK, (hd + 1) * RET_DK) for hd in range(RET_HEADS)]
        scores, cross = [], []
        for hd, cols in enumerate(head_cols):
            qr = _rotate(q[:, cols], cos, sin).astype(BF16)
            kr = _rotate(k[:, cols], cos, sin) * (RET_DK ** -0.5)
            kd = (kr * _both_halves(kdec_ref, hd)).astype(BF16)
            scores.append(_dot_nt(qr, kr.astype(BF16)))
            s_old = s_ref[hd]
            cross.append(_dot(qr, s_old.astype(BF16)) * _both_halves(qdec_ref, hd))
            s_ref[hd] = s_old * cdec_ref[hd] + _dot_tn(kd, v[:, cols])
        for hd, cols in enumerate(head_cols):
            p = (scores[hd] * dmat_ref[hd]).astype(BF16)
            o = _dot(p, v[:, cols]) + cross[hd]
            on = _group_norm_gate(o, ggn_ref[:, cols], _silu(gr[:, cols]))
            mix_ref[rows, POOL_W + hd * RET_DV:POOL_W + (hd + 1) * RET_DV] = on.astype(mix_ref.dtype)

    @pl.when(t == pl.num_programs(1) - 1)
    def _():
        hist_ref[...] = ubuf_ref[0, T + 1:T + HIST_ROWS, :]

    ubuf_ref[0, :HIST_ROWS, :] = ubuf_ref[0, T:T + HIST_ROWS, :]

    for side in sides:
        side.tail()


def _mixer_sample_kernel(h_ref, cos_ref, sin_ref, dmat_ref, qdec_ref, kdec_ref, cdec_ref,
                         hist0_ref, s0_ref, gmix_ref, win_ref, wpool_ref, pscale_ref, ggn_ref,
                         mix_ref, hist_ref, s_ref,
                         ubuf_ref, qr_ref, kdt_ref, v_ref, oin_ref, ox_ref, gate_ref,
                         *, offset, seq_rows):
    rows = h_ref.shape[0]
    s = pl.program_id(0)

    @pl.when(s == 0)
    def _():
        n_seq = rows // seq_rows
        hn = _rmsnorm(h_ref[...], gmix_ref[...]).astype(BF16)

        def proj(j):
            return _dot(hn, win_ref[:, j * POOL_W:(j + 1) * POOL_W])

        ubuf_ref[:, :HIST_ROWS, :] = hist0_ref[...]
        ubuf_ref[:, HIST_ROWS:, :] = proj(0).reshape(n_seq, seq_rows, POOL_W)
        gp = proj(1)
        row = lax.broadcasted_iota(jnp.int32, (rows, POOL_GW), 0)
        pos = offset + lax.rem(row, seq_rows)
        pooled = _pool_inputs(ubuf_ref, pos, seq_rows)
        hist_ref[...] = ubuf_ref[:, seq_rows + 1:seq_rows + HIST_ROWS, :]
        q, k = proj(2), proj(3)
        _pool_outputs(pooled, gp, wpool_ref, pscale_ref, mix_ref)

        v_ref[...] = proj(4).astype(BF16)
        gate_ref[...] = _silu(proj(5))
        cos, sin = cos_ref[...], sin_ref[...]
        for hd in range(RET_HEADS):
            cols = slice(hd * RET_DK, (hd + 1) * RET_DK)
            qr = _rotate(q[:, cols], cos, sin).astype(BF16)
            kr = _rotate(k[:, cols], cos, sin) * (RET_DK ** -0.5)
            qr_ref[:, cols] = qr
            kdt_ref[hd] = (kr * _both_halves(kdec_ref, hd)).T.astype(BF16)
            p = (_dot_nt(qr, kr.astype(BF16)) * dmat_ref[hd]).astype(BF16)
            oin_ref[:, cols] = _dot(p, v_ref[:, cols])

    row_id = lax.broadcasted_iota(jnp.int32, (rows, RET_DV), 0)
    for i in range(s0_ref.shape[0]):
        r0 = pl.multiple_of((s * s0_ref.shape[0] + i) * seq_rows, seq_rows)
        in_seq = (row_id >= r0) & (row_id < r0 + seq_rows)
        for hd in range(RET_HEADS):
            cols = slice(hd * RET_DK, (hd + 1) * RET_DK)
            s_old = s0_ref[i, hd]
            ox_ref[pl.ds(r0, seq_rows), cols] = _dot(qr_ref[pl.ds(r0, seq_rows), cols],
                                                     s_old.astype(BF16))
            vh = v_ref[:, cols]
            v_seq = jnp.where(in_seq, vh, jnp.zeros_like(vh))
            s_ref[i, hd] = s_old * cdec_ref[hd] + _dot(kdt_ref[hd], v_seq)

    @pl.when(s == pl.num_programs(0) - 1)
    def _():
        for hd in range(RET_HEADS):
            cols = slice(hd * RET_DK, (hd + 1) * RET_DK)
            o = oin_ref[:, cols] + ox_ref[:, cols] * _both_halves(qdec_ref, hd)
            on = _group_norm_gate(o, ggn_ref[:, cols], gate_ref[:, cols])
            mix_ref[:, POOL_W + hd * RET_DV:POOL_W + (hd + 1) * RET_DV] = on.astype(mix_ref.dtype)


def _output_body(h_ref, mix_ref, p_ref, wout_ref, gple_ref, wpg_ref, wple_ref, gfin_ref,
                 out_ref, final_norm):
    h = h_ref[...] + _dot(mix_ref[...], wout_ref[...])
    hn = _rmsnorm(h, gple_ref[...]).astype(BF16)
    gate = jax.nn.sigmoid(_dot(hn, wpg_ref[...]))
    h = h + gate * _dot(p_ref[...].astype(BF16), wple_ref[...])
    if final_norm:
        h = _rmsnorm(h, gfin_ref[...])
    out_ref[...] = h


def _output_kernel(h_ref, mix_ref, p_ref, wout_ref, gple_ref, wpg_ref, wple_ref, gfin_ref,
                   out_ref, *, final_norm):
    _output_body(h_ref, mix_ref, p_ref, wout_ref, gple_ref, wpg_ref, wple_ref, gfin_ref,
                 out_ref, final_norm)


def _output_convert_kernel(h_ref, mix_ref, p_ref, wout_ref, gple_ref, wpg_ref, wple_ref, gfin_ref,
                           win_next_hbm, out_ref, win_next_bf16_hbm,
                           win_in, win_out, side_sems, *, final_norm, next_layer):
    side = _SideConvert(win_next_hbm, next_layer, win_next_bf16_hbm, win_in, win_out, side_sems,
                        pl.program_id(0), pl.num_programs(0))
    side.head()
    _output_body(h_ref, mix_ref, p_ref, wout_ref, gple_ref, wpg_ref, wple_ref, gfin_ref,
                 out_ref, final_norm)
    side.tail()


def _retention_tables(rows, seq_rows):
    lg = np.log(1.0 - 2.0 ** (-5.0 - np.arange(RET_HEADS, dtype=np.float64)))
    n = np.arange(rows)
    r = (n % seq_rows).astype(np.float64)
    same = (n // seq_rows)[:, None] == (n // seq_rows)[None, :]
    diff = r[:, None] - r[None, :]
    keep = same & (diff >= 0)
    dmat = np.where(keep[None], np.exp(np.maximum(diff, 0.0)[None] * lg[:, None, None]), 0.0)
    qdec = np.exp((r[None, :] + 1.0) * lg[:, None])
    kdec = np.exp((seq_rows - 1.0 - r[None, :]) * lg[:, None])
    cdec = np.exp(seq_rows * lg)
    bcast = lambda a: np.broadcast_to(a[:, :, None], (RET_HEADS, rows, HALF))
    return tuple(jnp.asarray(a, F32) for a in (dmat, bcast(qdec), bcast(kdec), cdec))


def _rope_tables(pos):
    inv = 1.0 / (ROPE_BASE ** (jnp.arange(HALF, dtype=F32) / HALF))
    ang = pos.astype(F32)[:, None] * inv[None, :]
    return jnp.cos(ang), jnp.sin(ang)


def _mixer_prompt(h, layer, offset, tables, rope, w, win_bf16):
    B, L, _ = h.shape
    T = ROW_TILE
    n_steps = B * (L // T)
    dmat, qdec, kdec, cdec = tables
    cos, sin = rope
    const = lambda *idx: (lambda b, t: idx)
    in_specs = [
        pl.BlockSpec((None, T, D_MODEL), lambda b, t: (b, t, 0)),
        pl.BlockSpec((T, HALF), lambda b, t: (t, 0)),
        pl.BlockSpec((T, HALF), lambda b, t: (t, 0)),
        _resident((RET_HEADS, RET_CHUNK, RET_CHUNK), const(0, 0, 0)),
        _resident((RET_HEADS, RET_CHUNK, HALF), const(0, 0, 0)),
        _resident((RET_HEADS, RET_CHUNK, HALF), const(0, 0, 0)),
        pl.BlockSpec(memory_space=pltpu.SMEM),
        _resident((None, 1, D_MODEL), const(layer, 0, 0)),
        _resident((D_MODEL, IN_W), const(0, 0)),
        _resident((None, len(POOL_WINDOWS), POOL_GW, POOL_GW), const(layer, 0, 0, 0)),
        _resident((None, 1, POOL_W), const(layer, 0, 0)),
        _resident((None, 1, RET_W), const(layer, 0, 0)),
        pl.BlockSpec(memory_space=pl.ANY),
        pl.BlockSpec(memory_space=pl.ANY),
    ]
    out_specs = [
        pl.BlockSpec((None, T, MIX_W), lambda b, t: (b, t, 0)),
        pl.BlockSpec((None, POOL_HIST, POOL_W), lambda b, t: (b, 0, 0)),
        pl.BlockSpec((None, RET_HEADS, RET_DK, RET_DV), lambda b, t: (b, 0, 0, 0)),
        pl.BlockSpec(memory_space=pl.ANY),
        pl.BlockSpec(memory_space=pl.ANY),
    ]
    out_shape = [
        jax.ShapeDtypeStruct((B, L, MIX_W), BF16),
        jax.ShapeDtypeStruct((B, POOL_HIST, POOL_W), F32),
        jax.ShapeDtypeStruct((B, RET_HEADS, RET_DK, RET_DV), F32),
        jax.ShapeDtypeStruct((MIX_W, D_MODEL), BF16),
        jax.ShapeDtypeStruct((D_MODEL, D_MODEL), BF16),
    ]
    scratch_shapes = [
        pltpu.VMEM((1, HIST_ROWS + T, POOL_W), F32),
        pltpu.VMEM((MIX_W // n_steps, D_MODEL), F32),
        pltpu.VMEM((MIX_W // n_steps, D_MODEL), BF16),
        pltpu.SemaphoreType.DMA((2,)),
        pltpu.VMEM((D_MODEL // n_steps, D_MODEL), F32),
        pltpu.VMEM((D_MODEL // n_steps, D_MODEL), BF16),
        pltpu.SemaphoreType.DMA((2,)),
    ]
    return pl.pallas_call(
        functools.partial(_mixer_prompt_kernel, offset=offset, layer=layer),
        grid=(B, L // T),
        in_specs=in_specs,
        out_specs=out_specs,
        out_shape=out_shape,
        scratch_shapes=scratch_shapes,
        compiler_params=pltpu.CompilerParams(
            dimension_semantics=("arbitrary", "arbitrary"),
            vmem_limit_bytes=VMEM_LIMIT_BYTES),
        name=f"mixer_prompt_l{layer}",
    )(h, cos, sin, dmat, qdec, kdec, cdec, w["g_mix"], win_bf16, w["w_pool"],
      w["pool_scale"], w["g_gn"], w["w_out"], w["w_pg"])


def _mixer_sample(h2d, hist0, s0, layer, offset, seq_rows, tables, rope, w, win_bf16):
    rows = h2d.shape[0]
    n_seq = rows // seq_rows
    dmat, qdec, kdec, cdec = tables
    cos, sin = rope
    const = lambda *idx: (lambda s: idx)
    in_specs = [
        _resident((rows, D_MODEL), const(0, 0)),
        _resident((rows, HALF), const(0, 0)),
        _resident((rows, HALF), const(0, 0)),
        _resident((RET_HEADS, rows, rows), const(0, 0, 0)),
        _resident((RET_HEADS, rows, HALF), const(0, 0, 0)),
        _resident((RET_HEADS, rows, HALF), const(0, 0, 0)),
        pl.BlockSpec(memory_space=pltpu.SMEM),
        _resident((None, n_seq, HIST_ROWS, POOL_W), const(layer, 0, 0, 0)),
        pl.BlockSpec((None, SEQS_PER_STEP, RET_HEADS, RET_DK, RET_DV),
                     lambda s: (layer, s, 0, 0, 0)),
        _resident((None, 1, D_MODEL), const(layer, 0, 0)),
        _resident((D_MODEL, IN_W), const(0, 0)),
        _resident((None, len(POOL_WINDOWS), POOL_GW, POOL_GW), const(layer, 0, 0, 0)),
        _resident((None, 1, POOL_W), const(layer, 0, 0)),
        _resident((None, 1, RET_W), const(layer, 0, 0)),
    ]
    out_specs = [
        pl.BlockSpec((rows, MIX_W), const(0, 0)),
        pl.BlockSpec((n_seq, POOL_HIST, POOL_W), const(0, 0, 0)),
        pl.BlockSpec((SEQS_PER_STEP, RET_HEADS, RET_DK, RET_DV), lambda s: (s, 0, 0, 0)),
    ]
    out_shape = [
        jax.ShapeDtypeStruct((rows, MIX_W), BF16),
        jax.ShapeDtypeStruct((n_seq, POOL_HIST, POOL_W), F32),
        jax.ShapeDtypeStruct((n_seq, RET_HEADS, RET_DK, RET_DV), F32),
    ]
    scratch_shapes = [
        pltpu.VMEM((n_seq, HIST_ROWS + seq_rows, POOL_W), F32),
        pltpu.VMEM((rows, RET_W), BF16),
        pltpu.VMEM((RET_HEADS, RET_DK, rows), BF16),
        pltpu.VMEM((rows, RET_W), BF16),
        pltpu.VMEM((rows, RET_W), F32),
        pltpu.VMEM((rows, RET_W), F32),
        pltpu.VMEM((rows, RET_W), F32),
    ]
    return pl.pallas_call(
        functools.partial(_mixer_sample_kernel, offset=offset, seq_rows=seq_rows),
        grid=(n_seq // SEQS_PER_STEP,),
        in_specs=in_specs,
        out_specs=out_specs,
        out_shape=out_shape,
        scratch_shapes=scratch_shapes,
        compiler_params=pltpu.CompilerParams(
            dimension_semantics=("arbitrary",),
            vmem_limit_bytes=VMEM_LIMIT_BYTES),
        name=f"mixer_sample_l{layer}",
    )(h2d, cos, sin, dmat, qdec, kdec, cdec, hist0, s0, w["g_mix"], win_bf16, w["w_pool"],
      w["pool_scale"], w["g_gn"])


def _output_call(h2d, mix2d, p2d, layer, w, wout_bf16, wpg_bf16, tag, convert_next):
    rows = h2d.shape[0]
    T = min(OUT_ROW_TILE, rows)
    n_steps = rows // T
    final_norm = layer == DEPTH - 1
    const = lambda *idx: (lambda i: idx)
    in_specs = [
        pl.BlockSpec((T, D_MODEL), lambda i: (i, 0)),
        pl.BlockSpec((T, MIX_W), lambda i: (i, 0)),
        pl.BlockSpec((None, T, PLE_DIM), lambda i: (layer, i, 0)),
        _resident((MIX_W, D_MODEL), const(0, 0)),
        _resident((None, 1, D_MODEL), const(layer, 0, 0)),
        _resident((D_MODEL, D_MODEL), const(0, 0)),
        _resident((None, PLE_DIM, D_MODEL), const(layer, 0, 0)),
        _resident((1, D_MODEL), const(0, 0)),
    ]
    out_specs = [pl.BlockSpec((T, D_MODEL), lambda i: (i, 0))]
    out_shape = [jax.ShapeDtypeStruct((rows, D_MODEL), F32)]
    operands = [h2d, mix2d, p2d, wout_bf16, w["g_ple"], wpg_bf16, w["w_ple"], w["g_final"]]
    scratch_shapes = []
    if convert_next:
        body = functools.partial(_output_convert_kernel, final_norm=final_norm,
                                 next_layer=layer + 1)
        in_specs.append(pl.BlockSpec(memory_space=pl.ANY))
        operands.append(w["w_in"])
        out_specs.append(pl.BlockSpec(memory_space=pl.ANY))
        out_shape.append(jax.ShapeDtypeStruct((D_MODEL, IN_W), BF16))
        scratch_shapes = [
            pltpu.VMEM((D_MODEL // n_steps, IN_W), F32),
            pltpu.VMEM((D_MODEL // n_steps, IN_W), BF16),
            pltpu.SemaphoreType.DMA((2,)),
        ]
    else:
        body = functools.partial(_output_kernel, final_norm=final_norm)
    return pl.pallas_call(
        body,
        grid=(n_steps,),
        in_specs=in_specs,
        out_specs=out_specs,
        out_shape=out_shape,
        scratch_shapes=scratch_shapes,
        compiler_params=pltpu.CompilerParams(
            dimension_semantics=("arbitrary",),
            vmem_limit_bytes=VMEM_LIMIT_BYTES),
        name=f"output_{tag}_l{layer}",
    )(*operands)


def _prompt_trunk(x, p, w):
    B, L, _ = x.shape
    tables = _retention_tables(RET_CHUNK, RET_CHUNK)
    rope = _rope_tables(jnp.arange(L))
    p2d = p.reshape(DEPTH, B * L, PLE_DIM)
    h = x
    hists, states, w_bf16 = [], [], []
    win_b = w["w_in"][0].astype(BF16)
    for layer in range(DEPTH):
        mix, hist, state, wout_b, wpg_b = _mixer_prompt(h, layer, 0, tables, rope, w, win_b)
        w_bf16.append((win_b, wout_b, wpg_b))
        outs = _output_call(h.reshape(B * L, D_MODEL), mix.reshape(B * L, MIX_W), p2d, layer, w,
                            wout_b, wpg_b, "prompt", convert_next=layer + 1 < DEPTH)
        h = outs[0].reshape(B, L, D_MODEL)
        if layer + 1 < DEPTH:
            win_b = outs[1]
        hists.append(hist)
        states.append(state)
    return h, jnp.stack(hists, axis=0), jnp.stack(states, axis=0), w_bf16


def _sample_trunk(x, p, hist0, s0, offset, w, w_bf16):
    B, L, _ = x.shape
    rows = B * L
    tables = _retention_tables(rows, L)
    rope = _rope_tables(offset + jnp.arange(rows) % L)
    p2d = p.reshape(DEPTH, rows, PLE_DIM)
    hist0 = jnp.pad(hist0, ((0, 0), (0, 0), (HIST_ROWS - POOL_HIST, 0), (0, 0)))
    h = x.reshape(rows, D_MODEL)
    hists, states = [], []
    for layer in range(DEPTH):
        win_b, wout_b, wpg_b = w_bf16[layer]
        mix, hist, state = _mixer_sample(h, hist0, s0, layer, offset, L, tables, rope, w, win_b)
        h = _output_call(h, mix, p2d, layer, w, wout_b, wpg_b, "sample", convert_next=False)[0]
        hists.append(hist)
        states.append(state)
    return h.reshape(B, L, D_MODEL), jnp.stack(hists, axis=0), jnp.stack(states, axis=0)


def kernel(x_prompt, x_sample, p_prompt, p_sample, state_pool, state_ret, g_mix, w_in, w_pool,
           pool_scale, g_gn, w_out, g_ple, w_pg, w_ple, g_final):
    past_len = 1024
    row = lambda a: a.reshape(a.shape[0], 1, a.shape[1])
    w = {
        "g_mix": row(g_mix), "w_in": w_in, "w_pool": w_pool.astype(BF16),
        "pool_scale": row(pool_scale), "g_gn": row(g_gn), "w_out": w_out,
        "g_ple": row(g_ple), "w_pg": w_pg, "w_ple": w_ple.astype(BF16),
        "g_final": g_final.reshape(1, D_MODEL),
    }
    y_prompt, pool_prompt, ret_prompt, w_bf16 = _prompt_trunk(x_prompt, p_prompt, w)
    y_sample, pool_sample, ret_sample = _sample_trunk(
        x_sample, p_sample, state_pool, state_ret, past_len, w, w_bf16)
    return (y_prompt, y_sample, pool_prompt, ret_prompt, pool_sample, ret_sample)
```

```python
import functools

import jax
import jax.numpy as jnp
import numpy as np
from jax import lax
from jax.experimental import pallas as pl
from jax.experimental.pallas import tpu as pltpu

D_MODEL = 2048
DEPTH = 4
PLE_DIM = 256
POOL_W = 1024
POOL_WINDOWS = (2, 4, 8, 16)
POOL_GW = POOL_W // len(POOL_WINDOWS)
POOL_HIST = max(POOL_WINDOWS) - 1
HIST_ROWS = POOL_HIST + 1
RET_HEADS = 4
RET_DK = 256
RET_DV = 256
RET_W = RET_HEADS * RET_DV
MIX_W = POOL_W + RET_W
IN_W = 2 * POOL_W + 2 * RET_HEADS * RET_DK + 2 * RET_W
ROPE_BASE = 10000.0
EPS = 1e-6
GN_EPS = 1e-5
HALF = RET_DK // 2

RET_CHUNK = 256
ROW_TILE = 512
OUT_ROW_TILE = 512
SEQS_PER_STEP = 2
VMEM_LIMIT_BYTES = 56 * 1024 * 1024

F32 = jnp.float32
BF16 = jnp.bfloat16


def _resident(block_shape, index_map):
    return pl.BlockSpec(block_shape, index_map, pipeline_mode=pl.Buffered(1))


def _rmsnorm(x, g):
    ms = jnp.mean(x * x, axis=-1, keepdims=True)
    return x * lax.rsqrt(ms + EPS) * g


def _silu(x):
    return x * jax.nn.sigmoid(x)


def _dot(a, b):
    return jnp.dot(a, b, preferred_element_type=F32)


def _dot_nt(a, b):
    return lax.dot_general(a, b, (((1,), (1,)), ((), ())), preferred_element_type=F32)


def _dot_tn(a, b):
    return lax.dot_general(a, b, (((0,), (0,)), ((), ())), preferred_element_type=F32)


def _rotate(x, cos, sin):
    x1, x2 = x[:, :HALF], x[:, HALF:]
    return jnp.concatenate([x1 * cos - x2 * sin, x1 * sin + x2 * cos], axis=-1)


def _both_halves(ref, hd):
    return jnp.concatenate([ref[hd], ref[hd]], axis=-1)


def _pool_inputs(ubuf_ref, pos, seq_rows):
    n_seq = ubuf_ref.shape[0]
    rows = n_seq * seq_rows
    pooled = []
    for g, w in enumerate(POOL_WINDOWS):
        cols = slice(g * POOL_GW, (g + 1) * POOL_GW)
        u_g = ubuf_ref[:, HIST_ROWS:HIST_ROWS + seq_rows, cols]
        acc = u_g
        for j in range(1, w):
            acc = acc + ubuf_ref[:, HIST_ROWS - j:HIST_ROWS - j + seq_rows, cols]
        acc = acc.reshape(rows, POOL_GW)
        u_g = u_g.reshape(rows, POOL_GW)
        cnt = jnp.minimum(pos + 1, w).astype(F32)
        pooled.append((acc / cnt - u_g).astype(BF16))
    return pooled


def _pool_outputs(pooled, gp, wpool_ref, pscale_ref, mix_ref):
    for g in range(len(POOL_WINDOWS)):
        cols = slice(g * POOL_GW, (g + 1) * POOL_GW)
        y = _dot(pooled[g], wpool_ref[g])
        y = y * pscale_ref[:, cols] * _silu(gp[:, cols])
        mix_ref[:, cols] = y.astype(mix_ref.dtype)


def _group_norm_gate(o, ggn, gate):
    mu = jnp.mean(o, axis=-1, keepdims=True)
    d = o - mu
    var = jnp.mean(d * d, axis=-1, keepdims=True)
    return d * lax.rsqrt(var + GN_EPS) * ggn * gate


class _SideConvert:
    def __init__(self, src_hbm, layer, dst_hbm, in_ref, out_ref, sems, step, n_steps):
        self.src, self.layer, self.dst = src_hbm, layer, dst_hbm
        self.in_ref, self.out_ref, self.sems = in_ref, out_ref, sems
        self.step, self.n_steps = step, n_steps
        self.rows = in_ref.shape[0]

    def _load(self, j):
        r0 = pl.multiple_of(j * self.rows, self.rows)
        return pltpu.make_async_copy(self.src.at[self.layer, pl.ds(r0, self.rows), :],
                                     self.in_ref, self.sems.at[0])

    def _store(self, j):
        r0 = pl.multiple_of(j * self.rows, self.rows)
        return pltpu.make_async_copy(self.out_ref, self.dst.at[pl.ds(r0, self.rows), :],
                                     self.sems.at[1])

    def head(self):
        @pl.when(self.step == 0)
        def _():
            self._load(0).start()

        @pl.when((self.step > 0) & (self.step < self.n_steps))
        def _():
            self._store(self.step - 1).wait()

        @pl.when(self.step < self.n_steps)
        def _():
            self._load(self.step).wait()
            self.out_ref[...] = self.in_ref[...].astype(self.out_ref.dtype)
            self._store(self.step).start()
            self._load(jnp.minimum(self.step + 1, self.n_steps - 1)).start()

    def tail(self):
        @pl.when(self.step == self.n_steps - 1)
        def _():
            self._store(self.step).wait()
            self._load(self.step).wait()


def _mixer_prompt_kernel(h_ref, cos_ref, sin_ref, dmat_ref, qdec_ref, kdec_ref, cdec_ref,
                         gmix_ref, win_ref, wpool_ref, pscale_ref, ggn_ref, wout_hbm, wpg_hbm,
                         mix_ref, hist_ref, s_ref, wout_bf16_hbm, wpg_bf16_hbm,
                         ubuf_ref, wout_in, wout_out, wout_sems, wpg_in, wpg_out, wpg_sems,
                         *, offset, layer):
    T = h_ref.shape[0]
    C = dmat_ref.shape[1]
    t = pl.program_id(1)
    step = pl.program_id(0) * pl.num_programs(1) + t
    n_steps = pl.num_programs(0) * pl.num_programs(1)
    sides = [
        _SideConvert(wout_hbm, layer, wout_bf16_hbm, wout_in, wout_out, wout_sems, step, n_steps),
        _SideConvert(wpg_hbm, layer, wpg_bf16_hbm, wpg_in, wpg_out, wpg_sems, step, n_steps),
    ]
    for side in sides:
        side.head()

    @pl.when(t == 0)
    def _():
        ubuf_ref[:, :HIST_ROWS, :] = jnp.zeros((1, HIST_ROWS, POOL_W), F32)
        s_ref[...] = jnp.zeros(s_ref.shape, F32)

    for c in range(T // C):
        rows = slice(c * C, (c + 1) * C)
        hn = _rmsnorm(h_ref[rows, :], gmix_ref[...]).astype(BF16)

        def proj(j, hn=hn):
            return _dot(hn, win_ref[:, j * POOL_W:(j + 1) * POOL_W])

        ubuf_ref[0, HIST_ROWS + c * C:HIST_ROWS + (c + 1) * C, :] = proj(0)
        gp = proj(1)
        pos = offset + t * T + c * C + lax.broadcasted_iota(jnp.int32, (C, POOL_GW), 0)
        pooled = _pool_inputs(ubuf_ref.at[:, c * C:HIST_ROWS + (c + 1) * C, :], pos, C)
        q, k, v, gr = proj(2), proj(3), proj(4).astype(BF16), proj(5)
        _pool_outputs(pooled, gp, wpool_ref, pscale_ref, mix_ref.at[rows, :])

        cos, sin = cos_ref[rows, :], sin_ref[rows, :]
        head_cols = [slice(hd * RET_DK, (hd + 1) * RET_DK) for hd in range(RET_HEADS)]
        scores, cross = [], []
        for hd, cols in enumerate(head_cols):
            qr = _rotate(q[:, cols], cos, sin).astype(BF16)
            kr = _rotate(k[:, cols], cos, sin) * (RET_DK ** -0.5)
            kd = (kr * _both_halves(kdec_ref, hd)).astype(BF16)
            scores.append(_dot_nt(qr, kr.astype(BF16)))
            s_old = s_ref[hd]
            cross.append(_dot(qr, s_old.astype(BF16)) * _both_halves(qdec_ref, hd))
            s_ref[hd] = s_old * cdec_ref[hd] + _dot_tn(kd, v[:, cols])
        for hd, cols in enumerate(head_cols):
            p = (scores[hd] * dmat_ref[hd]).astype(BF16)
            o = _dot(p, v[:, cols]) + cross[hd]
            on = _group_norm_gate(o, ggn_ref[:, cols], _silu(gr[:, cols]))
            mix_ref[rows, POOL_W + hd * RET_DV:POOL_W + (hd + 1) * RET_DV] = on.astype(mix_ref.dtype)

    @pl.when(t == pl.num_programs(1) - 1)
    def _():
        hist_ref[...] = ubuf_ref[0, T + 1:T + HIST_ROWS, :]

    ubuf_ref[0, :HIST_ROWS, :] = ubuf_ref[0, T:T + HIST_ROWS, :]

    for side in sides:
        side.tail()


def _mixer_sample_kernel(h_ref, cos_ref, sin_ref, dmat_ref, qdec_ref, kdec_ref, cdec_ref,
                         hist0_ref, s0_ref, gmix_ref, win_ref, wpool_ref, pscale_ref, ggn_ref,
                         mix_ref, hist_ref, s_ref,
                         ubuf_ref, qr_ref, kdt_ref, v_ref, oin_ref, ox_ref, gate_ref,
                         *, offset, seq_rows):
    rows = h_ref.shape[0]
    s = pl.program_id(0)

    @pl.when(s == 0)
    def _():
        n_seq = rows // seq_rows
        hn = _rmsnorm(h_ref[...], gmix_ref[...]).astype(BF16)

        def proj(j):
            return _dot(hn, win_ref[:, j * POOL_W:(j + 1) * POOL_W])

        ubuf_ref[:, :HIST_ROWS, :] = hist0_ref[...]
        ubuf_ref[:, HIST_ROWS:, :] = proj(0).reshape(n_seq, seq_rows, POOL_W)
        gp = proj(1)
        row = lax.broadcasted_iota(jnp.int32, (rows, POOL_GW), 0)
        pos = offset + lax.rem(row, seq_rows)
        pooled = _pool_inputs(ubuf_ref, pos, seq_rows)
        hist_ref[...] = ubuf_ref[:, seq_rows + 1:seq_rows + HIST_ROWS, :]
        q, k = proj(2), proj(3)
        _pool_outputs(pooled, gp, wpool_ref, pscale_ref, mix_ref)

        v_ref[...] = proj(4).astype(BF16)
        gate_ref[...] = _silu(proj(5))
        cos, sin = cos_ref[...], sin_ref[...]
        for hd in range(RET_HEADS):
            cols = slice(hd * RET_DK, (hd + 1) * RET_DK)
            qr = _rotate(q[:, cols], cos, sin).astype(BF16)
            kr = _rotate(k[:, cols], cos, sin) * (RET_DK ** -0.5)
            qr_ref[:, cols] = qr
            kdt_ref[hd] = (kr * _both_halves(kdec_ref, hd)).T.astype(BF16)
            p = (_dot_nt(qr, kr.astype(BF16)) * dmat_ref[hd]).astype(BF16)
            oin_ref[:, cols] = _dot(p, v_ref[:, cols])

    row_id = lax.broadcasted_iota(jnp.int32, (rows, RET_DV), 0)
    for i in range(s0_ref.shape[0]):
        r0 = pl.multiple_of((s * s0_ref.shape[0] + i) * seq_rows, seq_rows)
        in_seq = (row_id >= r0) & (row_id < r0 + seq_rows)
        for hd in range(RET_HEADS):
            cols = slice(hd * RET_DK, (hd + 1) * RET_DK)
            s_old = s0_ref[i, hd]
            ox_ref[pl.ds(r0, seq_rows), cols] = _dot(qr_ref[pl.ds(r0, seq_rows), cols],
                                                     s_old.astype(BF16))
            vh = v_ref[:, cols]
            v_seq = jnp.where(in_seq, vh, jnp.zeros_like(vh))
            s_ref[i, hd] = s_old * cdec_ref[hd] + _dot(kdt_ref[hd], v_seq)

    @pl.when(s == pl.num_programs(0) - 1)
    def _():
        for hd in range(RET_HEADS):
            cols = slice(hd * RET_DK, (hd + 1) * RET_DK)
            o = oin_ref[:, cols] + ox_ref[:, cols] * _both_halves(qdec_ref, hd)
            on = _group_norm_gate(o, ggn_ref[:, cols], gate_ref[:, cols])
            mix_ref[:, POOL_W + hd * RET_DV:POOL_W + (hd + 1) * RET_DV] = on.astype(mix_ref.dtype)


def _output_body(h_ref, mix_ref, p_ref, wout_ref, gple_ref, wpg_ref, wple_ref, gfin_ref,
                 out_ref, final_norm):
    h = h_ref[...] + _dot(mix_ref[...], wout_ref[...])
    hn = _rmsnorm(h, gple_ref[...]).astype(BF16)
    gate = jax.nn.sigmoid(_dot(hn, wpg_ref[...]))
    h = h + gate * _dot(p_ref[...].astype(BF16), wple_ref[...])
    if final_norm:
        h = _rmsnorm(h, gfin_ref[...])
    out_ref[...] = h


def _output_kernel(h_ref, mix_ref, p_ref, hs_ref, mixs_ref, ps_ref,
                   wout_ref, gple_ref, wpg_ref, wple_ref, gfin_ref, *rest,
                   final_norm, next_layer, n_prompt_steps):
    i = pl.program_id(0)
    side = None
    if next_layer is None:
        out_ref, outs_ref = rest
    else:
        win_next_hbm, out_ref, outs_ref, win_next_bf16_hbm, win_in, win_out, side_sems = rest
        side = _SideConvert(win_next_hbm, next_layer, win_next_bf16_hbm, win_in, win_out,
                            side_sems, i, n_prompt_steps)
        side.head()

    @pl.when(i < n_prompt_steps)
    def _():
        _output_body(h_ref, mix_ref, p_ref, wout_ref, gple_ref, wpg_ref, wple_ref, gfin_ref,
                     out_ref, final_norm)

    @pl.when(i == n_prompt_steps)
    def _():
        _output_body(hs_ref, mixs_ref, ps_ref, wout_ref, gple_ref, wpg_ref, wple_ref, gfin_ref,
                     outs_ref, final_norm)

    if side is not None:
        side.tail()


def _retention_tables(rows, seq_rows):
    lg = np.log(1.0 - 2.0 ** (-5.0 - np.arange(RET_HEADS, dtype=np.float64)))
    n = np.arange(rows)
    r = (n % seq_rows).astype(np.float64)
    same = (n // seq_rows)[:, None] == (n // seq_rows)[None, :]
    diff = r[:, None] - r[None, :]
    keep = same & (diff >= 0)
    dmat = np.where(keep[None], np.exp(np.maximum(diff, 0.0)[None] * lg[:, None, None]), 0.0)
    qdec = np.exp((r[None, :] + 1.0) * lg[:, None])
    kdec = np.exp((seq_rows - 1.0 - r[None, :]) * lg[:, None])
    cdec = np.exp(seq_rows * lg)
    bcast = lambda a: np.broadcast_to(a[:, :, None], (RET_HEADS, rows, HALF))
    return tuple(jnp.asarray(a, F32) for a in (dmat, bcast(qdec), bcast(kdec), cdec))


def _rope_tables(pos):
    inv = 1.0 / (ROPE_BASE ** (jnp.arange(HALF, dtype=F32) / HALF))
    ang = pos.astype(F32)[:, None] * inv[None, :]
    return jnp.cos(ang), jnp.sin(ang)


def _mixer_prompt(h, layer, offset, tables, rope, w, win_bf16):
    B, L, _ = h.shape
    T = ROW_TILE
    n_steps = B * (L // T)
    dmat, qdec, kdec, cdec = tables
    cos, sin = rope
    const = lambda *idx: (lambda b, t: idx)
    in_specs = [
        pl.BlockSpec((None, T, D_MODEL), lambda b, t: (b, t, 0)),
        pl.BlockSpec((T, HALF), lambda b, t: (t, 0)),
        pl.BlockSpec((T, HALF), lambda b, t: (t, 0)),
        _resident((RET_HEADS, RET_CHUNK, RET_CHUNK), const(0, 0, 0)),
        _resident((RET_HEADS, RET_CHUNK, HALF), const(0, 0, 0)),
        _resident((RET_HEADS, RET_CHUNK, HALF), const(0, 0, 0)),
        pl.BlockSpec(memory_space=pltpu.SMEM),
        _resident((None, 1, D_MODEL), const(layer, 0, 0)),
        _resident((D_MODEL, IN_W), const(0, 0)),
        _resident((None, len(POOL_WINDOWS), POOL_GW, POOL_GW), const(layer, 0, 0, 0)),
        _resident((None, 1, POOL_W), const(layer, 0, 0)),
        _resident((None, 1, RET_W), const(layer, 0, 0)),
        pl.BlockSpec(memory_space=pl.ANY),
        pl.BlockSpec(memory_space=pl.ANY),
    ]
    out_specs = [
        pl.BlockSpec((None, T, MIX_W), lambda b, t: (b, t, 0)),
        pl.BlockSpec((None, POOL_HIST, POOL_W), lambda b, t: (b, 0, 0)),
        pl.BlockSpec((None, RET_HEADS, RET_DK, RET_DV), lambda b, t: (b, 0, 0, 0)),
        pl.BlockSpec(memory_space=pl.ANY),
        pl.BlockSpec(memory_space=pl.ANY),
    ]
    out_shape = [
        jax.ShapeDtypeStruct((B, L, MIX_W), BF16),
        jax.ShapeDtypeStruct((B, POOL_HIST, POOL_W), F32),
        jax.ShapeDtypeStruct((B, RET_HEADS, RET_DK, RET_DV), F32),
        jax.ShapeDtypeStruct((MIX_W, D_MODEL), BF16),
        jax.ShapeDtypeStruct((D_MODEL, D_MODEL), BF16),
    ]
    scratch_shapes = [
        pltpu.VMEM((1, HIST_ROWS + T, POOL_W), F32),
        pltpu.VMEM((MIX_W // n_steps, D_MODEL), F32),
        pltpu.VMEM((MIX_W // n_steps, D_MODEL), BF16),
        pltpu.SemaphoreType.DMA((2,)),
        pltpu.VMEM((D_MODEL // n_steps, D_MODEL), F32),
        pltpu.VMEM((D_MODEL // n_steps, D_MODEL), BF16),
        pltpu.SemaphoreType.DMA((2,)),
    ]
    return pl.pallas_call(
        functools.partial(_mixer_prompt_kernel, offset=offset, layer=layer),
        grid=(B, L // T),
        in_specs=in_specs,
        out_specs=out_specs,
        out_shape=out_shape,
        scratch_shapes=scratch_shapes,
        compiler_params=pltpu.CompilerParams(
            dimension_semantics=("arbitrary", "arbitrary"),
            vmem_limit_bytes=VMEM_LIMIT_BYTES),
        name=f"mixer_prompt_l{layer}",
    )(h, cos, sin, dmat, qdec, kdec, cdec, w["g_mix"], win_bf16, w["w_pool"],
      w["pool_scale"], w["g_gn"], w["w_out"], w["w_pg"])


def _mixer_sample(h2d, hist0, s0, layer, offset, seq_rows, tables, rope, w, win_bf16):
    rows = h2d.shape[0]
    n_seq = rows // seq_rows
    dmat, qdec, kdec, cdec = tables
    cos, sin = rope
    const = lambda *idx: (lambda s: idx)
    in_specs = [
        _resident((rows, D_MODEL), const(0, 0)),
        _resident((rows, HALF), const(0, 0)),
        _resident((rows, HALF), const(0, 0)),
        _resident((RET_HEADS, rows, rows), const(0, 0, 0)),
        _resident((RET_HEADS, rows, HALF), const(0, 0, 0)),
        _resident((RET_HEADS, rows, HALF), const(0, 0, 0)),
        pl.BlockSpec(memory_space=pltpu.SMEM),
        _resident((None, n_seq, HIST_ROWS, POOL_W), const(layer, 0, 0, 0)),
        pl.BlockSpec((None, SEQS_PER_STEP, RET_HEADS, RET_DK, RET_DV),
                     lambda s: (layer, s, 0, 0, 0)),
        _resident((None, 1, D_MODEL), const(layer, 0, 0)),
        _resident((D_MODEL, IN_W), const(0, 0)),
        _resident((None, len(POOL_WINDOWS), POOL_GW, POOL_GW), const(layer, 0, 0, 0)),
        _resident((None, 1, POOL_W), const(layer, 0, 0)),
        _resident((None, 1, RET_W), const(layer, 0, 0)),
    ]
    out_specs = [
        pl.BlockSpec((rows, MIX_W), const(0, 0)),
        pl.BlockSpec((n_seq, POOL_HIST, POOL_W), const(0, 0, 0)),
        pl.BlockSpec((SEQS_PER_STEP, RET_HEADS, RET_DK, RET_DV), lambda s: (s, 0, 0, 0)),
    ]
    out_shape = [
        jax.ShapeDtypeStruct((rows, MIX_W), BF16),
        jax.ShapeDtypeStruct((n_seq, POOL_HIST, POOL_W), F32),
        jax.ShapeDtypeStruct((n_seq, RET_HEADS, RET_DK, RET_DV), F32),
    ]
    scratch_shapes = [
        pltpu.VMEM((n_seq, HIST_ROWS + seq_rows, POOL_W), F32),
        pltpu.VMEM((rows, RET_W), BF16),
        pltpu.VMEM((RET_HEADS, RET_DK, rows), BF16),
        pltpu.VMEM((rows, RET_W), BF16),
        pltpu.VMEM((rows, RET_W), F32),
        pltpu.VMEM((rows, RET_W), F32),
        pltpu.VMEM((rows, RET_W), F32),
    ]
    return pl.pallas_call(
        functools.partial(_mixer_sample_kernel, offset=offset, seq_rows=seq_rows),
        grid=(n_seq // SEQS_PER_STEP,),
        in_specs=in_specs,
        out_specs=out_specs,
        out_shape=out_shape,
        scratch_shapes=scratch_shapes,
        compiler_params=pltpu.CompilerParams(
            dimension_semantics=("arbitrary",),
            vmem_limit_bytes=VMEM_LIMIT_BYTES),
        name=f"mixer_sample_l{layer}",
    )(h2d, cos, sin, dmat, qdec, kdec, cdec, hist0, s0, w["g_mix"], win_bf16, w["w_pool"],
      w["pool_scale"], w["g_gn"])


def _output_call(h_p, mix_p, p2d_p, h_s, mix_s, p2d_s, layer, w, wout_bf16, wpg_bf16):
    rows_p, rows_s = h_p.shape[0], h_s.shape[0]
    T = min(OUT_ROW_TILE, rows_p)
    n = rows_p // T
    final_norm = layer == DEPTH - 1
    next_layer = None if final_norm else layer + 1
    const = lambda *idx: (lambda i: idx)
    tile = lambda i: jnp.minimum(i, n - 1)
    in_specs = [
        pl.BlockSpec((T, D_MODEL), lambda i: (tile(i), 0)),
        pl.BlockSpec((T, MIX_W), lambda i: (tile(i), 0)),
        pl.BlockSpec((None, T, PLE_DIM), lambda i: (layer, tile(i), 0)),
        _resident((rows_s, D_MODEL), const(0, 0)),
        _resident((rows_s, MIX_W), const(0, 0)),
        _resident((None, rows_s, PLE_DIM), const(layer, 0, 0)),
        _resident((MIX_W, D_MODEL), const(0, 0)),
        _resident((None, 1, D_MODEL), const(layer, 0, 0)),
        _resident((D_MODEL, D_MODEL), const(0, 0)),
        _resident((None, PLE_DIM, D_MODEL), const(layer, 0, 0)),
        _resident((1, D_MODEL), const(0, 0)),
    ]
    out_specs = [pl.BlockSpec((T, D_MODEL), lambda i: (tile(i), 0)),
                 pl.BlockSpec((rows_s, D_MODEL), const(0, 0))]
    out_shape = [jax.ShapeDtypeStruct((rows_p, D_MODEL), F32),
                 jax.ShapeDtypeStruct((rows_s, D_MODEL), F32)]
    operands = [h_p, mix_p, p2d_p, h_s, mix_s, p2d_s,
                wout_bf16, w["g_ple"], wpg_bf16, w["w_ple"], w["g_final"]]
    scratch_shapes = []
    if next_layer is not None:
        in_specs.append(pl.BlockSpec(memory_space=pl.ANY))
        operands.append(w["w_in"])
        out_specs.append(pl.BlockSpec(memory_space=pl.ANY))
        out_shape.append(jax.ShapeDtypeStruct((D_MODEL, IN_W), BF16))
        scratch_shapes = [
            pltpu.VMEM((D_MODEL // n, IN_W), F32),
            pltpu.VMEM((D_MODEL // n, IN_W), BF16),
            pltpu.SemaphoreType.DMA((2,)),
        ]
    return pl.pallas_call(
        functools.partial(_output_kernel, final_norm=final_norm, next_layer=next_layer,
                          n_prompt_steps=n),
        grid=(n + 1,),
        in_specs=in_specs,
        out_specs=out_specs,
        out_shape=out_shape,
        scratch_shapes=scratch_shapes,
        compiler_params=pltpu.CompilerParams(
            dimension_semantics=("arbitrary",),
            vmem_limit_bytes=VMEM_LIMIT_BYTES),
        name=f"output_l{layer}",
    )(*operands)


def _trunks(x_p, p_p, x_s, p_s, hist0_s, s0_s, offset_s, w):
    B, L, _ = x_p.shape
    Bs, Ls, _ = x_s.shape
    rows_p, rows_s = B * L, Bs * Ls
    tables_p = _retention_tables(RET_CHUNK, RET_CHUNK)
    tables_s = _retention_tables(rows_s, Ls)
    rope_p = _rope_tables(jnp.arange(L))
    rope_s = _rope_tables(offset_s + jnp.arange(rows_s) % Ls)
    p2d_p = p_p.reshape(DEPTH, rows_p, PLE_DIM)
    p2d_s = p_s.reshape(DEPTH, rows_s, PLE_DIM)
    hist0_s = jnp.pad(hist0_s, ((0, 0), (0, 0), (HIST_ROWS - POOL_HIST, 0), (0, 0)))
    h_p, h_s = x_p, x_s.reshape(rows_s, D_MODEL)
    hists_p, states_p, hists_s, states_s = [], [], [], []
    win_b = w["w_in"][0].astype(BF16)
    for layer in range(DEPTH):
        mix_p, hist, state, wout_b, wpg_b = _mixer_prompt(h_p, layer, 0, tables_p, rope_p, w, win_b)
        hists_p.append(hist)
        states_p.append(state)
        mix_s, hist, state = _mixer_sample(h_s, hist0_s, s0_s, layer, offset_s, Ls, tables_s,
                                           rope_s, w, win_b)
        hists_s.append(hist)
        states_s.append(state)
        outs = _output_call(h_p.reshape(rows_p, D_MODEL), mix_p.reshape(rows_p, MIX_W), p2d_p,
                            h_s, mix_s, p2d_s, layer, w, wout_b, wpg_b)
        h_p, h_s = outs[0].reshape(B, L, D_MODEL), outs[1]
        if layer + 1 < DEPTH:
            win_b = outs[2]
    stack = lambda xs: jnp.stack(xs, axis=0)
    return (h_p, h_s.reshape(Bs, Ls, D_MODEL), stack(hists_p), stack(states_p), stack(hists_s),
            stack(states_s))


def kernel(x_prompt, x_sample, p_prompt, p_sample, state_pool, state_ret, g_mix, w_in, w_pool,
           pool_scale, g_gn, w_out, g_ple, w_pg, w_ple, g_final):
    past_len = 1024
    row = lambda a: a.reshape(a.shape[0], 1, a.shape[1])
    w = {
        "g_mix": row(g_mix), "w_in": w_in, "w_pool": w_pool.astype(BF16),
        "pool_scale": row(pool_scale), "g_gn": row(g_gn), "w_out": w_out,
        "g_ple": row(g_ple), "w_pg": w_pg, "w_ple": w_ple.astype(BF16),
        "g_final": g_final.reshape(1, D_MODEL),
    }
    return _trunks(x_prompt, p_prompt, x_sample, p_sample, state_pool, state_ret, past_len, w)
```

```python
import functools

import jax
import jax.numpy as jnp
import numpy as np
from jax import lax
from jax.experimental import pallas as pl
from jax.experimental.pallas import tpu as pltpu

D_MODEL = 2048
DEPTH = 4
PLE_DIM = 256
POOL_W = 1024
POOL_WINDOWS = (2, 4, 8, 16)
POOL_GW = POOL_W // len(POOL_WINDOWS)
POOL_HIST = max(POOL_WINDOWS) - 1
HIST_ROWS = POOL_HIST + 1
RET_HEADS = 4
RET_DK = 256
RET_DV = 256
RET_W = RET_HEADS * RET_DV
MIX_W = POOL_W + RET_W
IN_W = 2 * POOL_W + 2 * RET_HEADS * RET_DK + 2 * RET_W
ROPE_BASE = 10000.0
EPS = 1e-6
GN_EPS = 1e-5
HALF = RET_DK // 2

RET_CHUNK = 256
ROW_TILE = 512
OUT_ROW_TILE = 512
SEQS_PER_STEP = 2
VMEM_LIMIT_BYTES = 56 * 1024 * 1024

F32 = jnp.float32
BF16 = jnp.bfloat16


def _resident(block_shape, index_map):
    return pl.BlockSpec(block_shape, index_map, pipeline_mode=pl.Buffered(1))


def _rmsnorm(x, g):
    ms = jnp.mean(x * x, axis=-1, keepdims=True)
    return x * lax.rsqrt(ms + EPS) * g


def _silu(x):
    return x * jax.nn.sigmoid(x)


def _dot(a, b):
    return jnp.dot(a, b, preferred_element_type=F32)


def _dot_nt(a, b):
    return lax.dot_general(a, b, (((1,), (1,)), ((), ())), preferred_element_type=F32)


def _dot_tn(a, b):
    return lax.dot_general(a, b, (((0,), (0,)), ((), ())), preferred_element_type=F32)


def _rotate(x, cos, sin):
    x1, x2 = x[:, :HALF], x[:, HALF:]
    return jnp.concatenate([x1 * cos - x2 * sin, x1 * sin + x2 * cos], axis=-1)


def _both_halves(ref, hd):
    return jnp.concatenate([ref[hd], ref[hd]], axis=-1)


def _pool_inputs(ubuf_ref, pos, seq_rows):
    n_seq = ubuf_ref.shape[0]
    rows = n_seq * seq_rows
    pooled = []
    for g, w in enumerate(POOL_WINDOWS):
        cols = slice(g * POOL_GW, (g + 1) * POOL_GW)
        u_g = ubuf_ref[:, HIST_ROWS:HIST_ROWS + seq_rows, cols]
        acc = u_g
        for j in range(1, w):
            acc = acc + ubuf_ref[:, HIST_ROWS - j:HIST_ROWS - j + seq_rows, cols]
        acc = acc.reshape(rows, POOL_GW)
        u_g = u_g.reshape(rows, POOL_GW)
        cnt = jnp.minimum(pos + 1, w).astype(F32)
        pooled.append((acc / cnt - u_g).astype(BF16))
    return pooled


def _pool_outputs(pooled, gp, wpool_ref, pscale_ref, mix_ref):
    for g in range(len(POOL_WINDOWS)):
        cols = slice(g * POOL_GW, (g + 1) * POOL_GW)
        y = _dot(pooled[g], wpool_ref[g])
        y = y * pscale_ref[:, cols] * _silu(gp[:, cols])
        mix_ref[:, cols] = y.astype(mix_ref.dtype)


def _group_norm_gate(o, ggn, gate):
    mu = jnp.mean(o, axis=-1, keepdims=True)
    d = o - mu
    var = jnp.mean(d * d, axis=-1, keepdims=True)
    return d * lax.rsqrt(var + GN_EPS) * ggn * gate


class _SideConvert:
    def __init__(self, src_hbm, layer, dst_hbm, in_ref, out_ref, sems, step, n_steps):
        self.src, self.layer, self.dst = src_hbm, layer, dst_hbm
        self.in_ref, self.out_ref, self.sems = in_ref, out_ref, sems
        self.step, self.n_steps = step, n_steps
        self.rows = in_ref.shape[0]

    def _load(self, j):
        r0 = pl.multiple_of(j * self.rows, self.rows)
        return pltpu.make_async_copy(self.src.at[self.layer, pl.ds(r0, self.rows), :],
                                     self.in_ref, self.sems.at[0])

    def _store(self, j):
        r0 = pl.multiple_of(j * self.rows, self.rows)
        return pltpu.make_async_copy(self.out_ref, self.dst.at[pl.ds(r0, self.rows), :],
                                     self.sems.at[1])

    def head(self):
        @pl.when(self.step == 0)
        def _():
            self._load(0).start()

        @pl.when((self.step > 0) & (self.step < self.n_steps))
        def _():
            self._store(self.step - 1).wait()

        @pl.when(self.step < self.n_steps)
        def _():
            self._load(self.step).wait()
            self.out_ref[...] = self.in_ref[...].astype(self.out_ref.dtype)
            self._store(self.step).start()
            self._load(jnp.minimum(self.step + 1, self.n_steps - 1)).start()

    def tail(self):
        @pl.when(self.step == self.n_steps - 1)
        def _():
            self._store(self.step).wait()
            self._load(self.step).wait()


def _mixer_prompt_kernel(h_ref, cos_ref, sin_ref, dmat_ref, qdec_ref, kdec_ref, cdec_ref,
                         gmix_ref, win_ref, wpool_ref, pscale_ref, ggn_ref, wout_hbm, wpg_hbm,
                         mix_ref, hist_ref, s_ref, wout_bf16_hbm, wpg_bf16_hbm,
                         ubuf_ref, wout_in, wout_out, wout_sems, wpg_in, wpg_out, wpg_sems,
                         *, offset, layer):
    T = h_ref.shape[0]
    C = dmat_ref.shape[1]
    t = pl.program_id(1)
    step = pl.program_id(0) * pl.num_programs(1) + t
    n_steps = pl.num_programs(0) * pl.num_programs(1)
    sides = [
        _SideConvert(wout_hbm, layer, wout_bf16_hbm, wout_in, wout_out, wout_sems, step, n_steps),
        _SideConvert(wpg_hbm, layer, wpg_bf16_hbm, wpg_in, wpg_out, wpg_sems, step, n_steps),
    ]
    for side in sides:
        side.head()

    @pl.when(t == 0)
    def _():
        ubuf_ref[:, :HIST_ROWS, :] = jnp.zeros((1, HIST_ROWS, POOL_W), F32)
        s_ref[...] = jnp.zeros(s_ref.shape, F32)

    for c in range(T // C):
        rows = slice(c * C, (c + 1) * C)
        hn = _rmsnorm(h_ref[rows, :], gmix_ref[...]).astype(BF16)

        def proj(j, hn=hn):
            return _dot(hn, win_ref[:, j * POOL_W:(j + 1) * POOL_W])

        ubuf_ref[0, HIST_ROWS + c * C:HIST_ROWS + (c + 1) * C, :] = proj(0)
        gp = proj(1)
        pos = offset + t * T + c * C + lax.broadcasted_iota(jnp.int32, (C, POOL_GW), 0)
        pooled = _pool_inputs(ubuf_ref.at[:, c * C:HIST_ROWS + (c + 1) * C, :], pos, C)
        q, k, v, gr = proj(2), proj(3), proj(4).astype(BF16), proj(5)
        _pool_outputs(pooled, gp, wpool_ref, pscale_ref, mix_ref.at[rows, :])

        cos, sin = cos_ref[rows, :], sin_ref[rows, :]
        head_cols = [slice(hd * RET_DK, (hd + 1) * RET_DK) for hd in range(RET_HEADS)]
        scores, cross = [], []
        for hd, cols in enumerate(head_cols):
            qr = _rotate(q[:, cols], cos, sin).astype(BF16)
            kr = _rotate(k[:, cols], cos, sin) * (RET_DK ** -0.5)
            kd = (kr * _both_halves(kdec_ref, hd)).astype(BF16)
            scores.append(_dot_nt(qr, kr.astype(BF16)))
            s_old = s_ref[hd]
            cross.append(_dot(qr, s_old.astype(BF16)) * _both_halves(qdec_ref, hd))
            s_ref[hd] = s_old * cdec_ref[hd] + _dot_tn(kd, v[:, cols])
        for hd, cols in enumerate(head_cols):
            p = (scores[hd] * dmat_ref[hd]).astype(BF16)
            o = _dot(p, v[:, cols]) + cross[hd]
            on = _group_norm_gate(o, ggn_ref[:, cols], _silu(gr[:, cols]))
            mix_ref[rows, POOL_W + hd * RET_DV:POOL_W + (hd + 1) * RET_DV] = on.astype(mix_ref.dtype)

    @pl.when(t == pl.num_programs(1) - 1)
    def _():
        hist_ref[...] = ubuf_ref[0, T + 1:T + HIST_ROWS, :]

    ubuf_ref[0, :HIST_ROWS, :] = ubuf_ref[0, T:T + HIST_ROWS, :]

    for side in sides:
        side.tail()


def _mixer_sample_kernel(h_ref, cos_ref, sin_ref, dmat_ref, qdec_ref, kdec_ref, cdec_ref,
                         hist0_ref, s0_ref, gmix_ref, win_ref, wpool_ref, pscale_ref, ggn_ref,
                         mix_ref, hist_ref, s_ref,
                         ubuf_ref, qr_ref, kdt_ref, v_ref, oin_ref, ox_ref, gate_ref,
                         *, offset, seq_rows):
    rows = h_ref.shape[0]
    s = pl.program_id(0)

    @pl.when(s == 0)
    def _():
        n_seq = rows // seq_rows
        hn = _rmsnorm(h_ref[...], gmix_ref[...]).astype(BF16)

        def proj(j):
            return _dot(hn, win_ref[:, j * POOL_W:(j + 1) * POOL_W])

        ubuf_ref[:, :HIST_ROWS, :] = hist0_ref[...]
        ubuf_ref[:, HIST_ROWS:, :] = proj(0).reshape(n_seq, seq_rows, POOL_W)
        gp = proj(1)
        row = lax.broadcasted_iota(jnp.int32, (rows, POOL_GW), 0)
        pos = offset + lax.rem(row, seq_rows)
        pooled = _pool_inputs(ubuf_ref, pos, seq_rows)
        hist_ref[...] = ubuf_ref[:, seq_rows + 1:seq_rows + HIST_ROWS, :]
        q, k = proj(2), proj(3)
        _pool_outputs(pooled, gp, wpool_ref, pscale_ref, mix_ref)

        v_ref[...] = proj(4).astype(BF16)
        gate_ref[...] = _silu(proj(5))
        cos, sin = cos_ref[...], sin_ref[...]
        for hd in range(RET_HEADS):
            cols = slice(hd * RET_DK, (hd + 1) * RET_DK)
            qr = _rotate(q[:, cols], cos, sin).astype(BF16)
            kr = _rotate(k[:, cols], cos, sin) * (RET_DK ** -0.5)
            qr_ref[:, cols] = qr
            kdt_ref[hd] = (kr * _both_halves(kdec_ref, hd)).T.astype(BF16)
            p = (_dot_nt(qr, kr.astype(BF16)) * dmat_ref[hd]).astype(BF16)
            oin_ref[:, cols] = _dot(p, v_ref[:, cols])

    row_id = lax.broadcasted_iota(jnp.int32, (rows, RET_DV), 0)
    for i in range(s0_ref.shape[0]):
        r0 = pl.multiple_of((s * s0_ref.shape[0] + i) * seq_rows, seq_rows)
        in_seq = (row_id >= r0) & (row_id < r0 + seq_rows)
        for hd in range(RET_HEADS):
            cols = slice(hd * RET_DK, (hd + 1) * RET_DK)
            s_old = s0_ref[i, hd]
            ox_ref[pl.ds(r0, seq_rows), cols] = _dot(qr_ref[pl.ds(r0, seq_rows), cols],
                                                     s_old.astype(BF16))
            vh = v_ref[:, cols]
            v_seq = jnp.where(in_seq, vh, jnp.zeros_like(vh))
            s_ref[i, hd] = s_old * cdec_ref[hd] + _dot(kdt_ref[hd], v_seq)

    @pl.when(s == pl.num_programs(0) - 1)
    def _():
        for hd in range(RET_HEADS):
            cols = slice(hd * RET_DK, (hd + 1) * RET_DK)
            o = oin_ref[:, cols] + ox_ref[:, cols] * _both_halves(qdec_ref, hd)
            on = _group_norm_gate(o, ggn_ref[:, cols], gate_ref[:, cols])
            mix_ref[:, POOL_W + hd * RET_DV:POOL_W + (hd + 1) * RET_DV] = on.astype(mix_ref.dtype)


def _output_body(h_ref, mix_ref, p_ref, wout_ref, gple_ref, wpg_ref, wple_ref, gfin_ref,
                 out_ref, final_norm):
    h = h_ref[...] + _dot(mix_ref[...], wout_ref[...])
    hn = _rmsnorm(h, gple_ref[...]).astype(BF16)
    gate = jax.nn.sigmoid(_dot(hn, wpg_ref[...]))
    h = h + gate * _dot(p_ref[...].astype(BF16), wple_ref[...])
    if final_norm:
        h = _rmsnorm(h, gfin_ref[...])
    out_ref[...] = h


def _output_kernel(h_ref, mix_ref, p_ref, hs_ref, mixs_ref, ps_ref,
                   wout_ref, gple_ref, wpg_ref, wple_ref, gfin_ref, *rest,
                   final_norm, next_layer, n_prompt_steps):
    i = pl.program_id(0)
    side = None
    if next_layer is None:
        out_ref, outs_ref = rest
    else:
        win_next_hbm, out_ref, outs_ref, win_next_bf16_hbm, win_in, win_out, side_sems = rest
        side = _SideConvert(win_next_hbm, next_layer, win_next_bf16_hbm, win_in, win_out,
                            side_sems, i, n_prompt_steps)
        side.head()

    @pl.when(i == n_prompt_steps)
    def _():
        _output_body(hs_ref, mixs_ref, ps_ref, wout_ref, gple_ref, wpg_ref, wple_ref, gfin_ref,
                     outs_ref, final_norm)

    @pl.when(i != n_prompt_steps)
    def _():
        _output_body(h_ref, mix_ref, p_ref, wout_ref, gple_ref, wpg_ref, wple_ref, gfin_ref,
                     out_ref, final_norm)

    if side is not None:
        side.tail()


def _retention_tables(rows, seq_rows):
    lg = np.log(1.0 - 2.0 ** (-5.0 - np.arange(RET_HEADS, dtype=np.float64)))
    n = np.arange(rows)
    r = (n % seq_rows).astype(np.float64)
    same = (n // seq_rows)[:, None] == (n // seq_rows)[None, :]
    diff = r[:, None] - r[None, :]
    keep = same & (diff >= 0)
    dmat = np.where(keep[None], np.exp(np.maximum(diff, 0.0)[None] * lg[:, None, None]), 0.0)
    qdec = np.exp((r[None, :] + 1.0) * lg[:, None])
    kdec = np.exp((seq_rows - 1.0 - r[None, :]) * lg[:, None])
    cdec = np.exp(seq_rows * lg)
    bcast = lambda a: np.broadcast_to(a[:, :, None], (RET_HEADS, rows, HALF))
    return tuple(jnp.asarray(a, F32) for a in (dmat, bcast(qdec), bcast(kdec), cdec))


def _rope_tables(pos):
    inv = 1.0 / (ROPE_BASE ** (jnp.arange(HALF, dtype=F32) / HALF))
    ang = pos.astype(F32)[:, None] * inv[None, :]
    return jnp.cos(ang), jnp.sin(ang)


def _mixer_prompt(h, layer, offset, tables, rope, w, win_bf16):
    B, L, _ = h.shape
    T = ROW_TILE
    n_steps = B * (L // T)
    dmat, qdec, kdec, cdec = tables
    cos, sin = rope
    const = lambda *idx: (lambda b, t: idx)
    in_specs = [
        pl.BlockSpec((None, T, D_MODEL), lambda b, t: (b, t, 0)),
        pl.BlockSpec((T, HALF), lambda b, t: (t, 0)),
        pl.BlockSpec((T, HALF), lambda b, t: (t, 0)),
        _resident((RET_HEADS, RET_CHUNK, RET_CHUNK), const(0, 0, 0)),
        _resident((RET_HEADS, RET_CHUNK, HALF), const(0, 0, 0)),
        _resident((RET_HEADS, RET_CHUNK, HALF), const(0, 0, 0)),
        pl.BlockSpec(memory_space=pltpu.SMEM),
        _resident((None, 1, D_MODEL), const(layer, 0, 0)),
        _resident((D_MODEL, IN_W), const(0, 0)),
        _resident((None, len(POOL_WINDOWS), POOL_GW, POOL_GW), const(layer, 0, 0, 0)),
        _resident((None, 1, POOL_W), const(layer, 0, 0)),
        _resident((None, 1, RET_W), const(layer, 0, 0)),
        pl.BlockSpec(memory_space=pl.ANY),
        pl.BlockSpec(memory_space=pl.ANY),
    ]
    out_specs = [
        pl.BlockSpec((None, T, MIX_W), lambda b, t: (b, t, 0)),
        pl.BlockSpec((None, POOL_HIST, POOL_W), lambda b, t: (b, 0, 0)),
        pl.BlockSpec((None, RET_HEADS, RET_DK, RET_DV), lambda b, t: (b, 0, 0, 0)),
        pl.BlockSpec(memory_space=pl.ANY),
        pl.BlockSpec(memory_space=pl.ANY),
    ]
    out_shape = [
        jax.ShapeDtypeStruct((B, L, MIX_W), BF16),
        jax.ShapeDtypeStruct((B, POOL_HIST, POOL_W), F32),
        jax.ShapeDtypeStruct((B, RET_HEADS, RET_DK, RET_DV), F32),
        jax.ShapeDtypeStruct((MIX_W, D_MODEL), BF16),
        jax.ShapeDtypeStruct((D_MODEL, D_MODEL), BF16),
    ]
    scratch_shapes = [
        pltpu.VMEM((1, HIST_ROWS + T, POOL_W), F32),
        pltpu.VMEM((MIX_W // n_steps, D_MODEL), F32),
        pltpu.VMEM((MIX_W // n_steps, D_MODEL), BF16),
        pltpu.SemaphoreType.DMA((2,)),
        pltpu.VMEM((D_MODEL // n_steps, D_MODEL), F32),
        pltpu.VMEM((D_MODEL // n_steps, D_MODEL), BF16),
        pltpu.SemaphoreType.DMA((2,)),
    ]
    return pl.pallas_call(
        functools.partial(_mixer_prompt_kernel, offset=offset, layer=layer),
        grid=(B, L // T),
        in_specs=in_specs,
        out_specs=out_specs,
        out_shape=out_shape,
        scratch_shapes=scratch_shapes,
        compiler_params=pltpu.CompilerParams(
            dimension_semantics=("arbitrary", "arbitrary"),
            vmem_limit_bytes=VMEM_LIMIT_BYTES),
        name=f"mixer_prompt_l{layer}",
    )(h, cos, sin, dmat, qdec, kdec, cdec, w["g_mix"], win_bf16, w["w_pool"],
      w["pool_scale"], w["g_gn"], w["w_out"], w["w_pg"])


def _mixer_sample(h2d, hist0, s0, layer, offset, seq_rows, tables, rope, w, win_bf16):
    rows = h2d.shape[0]
    n_seq = rows // seq_rows
    dmat, qdec, kdec, cdec = tables
    cos, sin = rope
    const = lambda *idx: (lambda s: idx)
    in_specs = [
        _resident((rows, D_MODEL), const(0, 0)),
        _resident((rows, HALF), const(0, 0)),
        _resident((rows, HALF), const(0, 0)),
        _resident((RET_HEADS, rows, rows), const(0, 0, 0)),
        _resident((RET_HEADS, rows, HALF), const(0, 0, 0)),
        _resident((RET_HEADS, rows, HALF), const(0, 0, 0)),
        pl.BlockSpec(memory_space=pltpu.SMEM),
        _resident((None, n_seq, HIST_ROWS, POOL_W), const(layer, 0, 0, 0)),
        pl.BlockSpec((None, SEQS_PER_STEP, RET_HEADS, RET_DK, RET_DV),
                     lambda s: (layer, s, 0, 0, 0)),
        _resident((None, 1, D_MODEL), const(layer, 0, 0)),
        _resident((D_MODEL, IN_W), const(0, 0)),
        _resident((None, len(POOL_WINDOWS), POOL_GW, POOL_GW), const(layer, 0, 0, 0)),
        _resident((None, 1, POOL_W), const(layer, 0, 0)),
        _resident((None, 1, RET_W), const(layer, 0, 0)),
    ]
    out_specs = [
        pl.BlockSpec((rows, MIX_W), const(0, 0)),
        pl.BlockSpec((n_seq, POOL_HIST, POOL_W), const(0, 0, 0)),
        pl.BlockSpec((SEQS_PER_STEP, RET_HEADS, RET_DK, RET_DV), lambda s: (s, 0, 0, 0)),
    ]
    out_shape = [
        jax.ShapeDtypeStruct((rows, MIX_W), BF16),
        jax.ShapeDtypeStruct((n_seq, POOL_HIST, POOL_W), F32),
        jax.ShapeDtypeStruct((n_seq, RET_HEADS, RET_DK, RET_DV), F32),
    ]
    scratch_shapes = [
        pltpu.VMEM((n_seq, HIST_ROWS + seq_rows, POOL_W), F32),
        pltpu.VMEM((rows, RET_W), BF16),
        pltpu.VMEM((RET_HEADS, RET_DK, rows), BF16),
        pltpu.VMEM((rows, RET_W), BF16),
        pltpu.VMEM((rows, RET_W), F32),
        pltpu.VMEM((rows, RET_W), F32),
        pltpu.VMEM((rows, RET_W), F32),
    ]
    return pl.pallas_call(
        functools.partial(_mixer_sample_kernel, offset=offset, seq_rows=seq_rows),
        grid=(n_seq // SEQS_PER_STEP,),
        in_specs=in_specs,
        out_specs=out_specs,
        out_shape=out_shape,
        scratch_shapes=scratch_shapes,
        compiler_params=pltpu.CompilerParams(
            dimension_semantics=("arbitrary",),
            vmem_limit_bytes=VMEM_LIMIT_BYTES),
        name=f"mixer_sample_l{layer}",
    )(h2d, cos, sin, dmat, qdec, kdec, cdec, hist0, s0, w["g_mix"], win_bf16, w["w_pool"],
      w["pool_scale"], w["g_gn"])


def _output_call(h_p, mix_p, p2d_p, h_s, mix_s, p2d_s, layer, w, wout_bf16, wpg_bf16):
    rows_p, rows_s = h_p.shape[0], h_s.shape[0]
    T = min(OUT_ROW_TILE, rows_p)
    n = rows_p // T
    final_norm = layer == DEPTH - 1
    next_layer = None if final_norm else layer + 1
    const = lambda *idx: (lambda i: idx)
    tile = lambda i: jnp.minimum(i, n - 1)
    in_specs = [
        pl.BlockSpec((T, D_MODEL), lambda i: (tile(i), 0)),
        pl.BlockSpec((T, MIX_W), lambda i: (tile(i), 0)),
        pl.BlockSpec((None, T, PLE_DIM), lambda i: (layer, tile(i), 0)),
        _resident((rows_s, D_MODEL), const(0, 0)),
        _resident((rows_s, MIX_W), const(0, 0)),
        _resident((None, rows_s, PLE_DIM), const(layer, 0, 0)),
        _resident((MIX_W, D_MODEL), const(0, 0)),
        _resident((None, 1, D_MODEL), const(layer, 0, 0)),
        _resident((D_MODEL, D_MODEL), const(0, 0)),
        _resident((None, PLE_DIM, D_MODEL), const(layer, 0, 0)),
        _resident((1, D_MODEL), const(0, 0)),
    ]
    out_specs = [pl.BlockSpec((T, D_MODEL), lambda i: (tile(i), 0)),
                 pl.BlockSpec((rows_s, D_MODEL), const(0, 0))]
    out_shape = [jax.ShapeDtypeStruct((rows_p, D_MODEL), F32),
                 jax.ShapeDtypeStruct((rows_s, D_MODEL), F32)]
    operands = [h_p, mix_p, p2d_p, h_s, mix_s, p2d_s,
                wout_bf16, w["g_ple"], wpg_bf16, w["w_ple"], w["g_final"]]
    scratch_shapes = []
    if next_layer is not None:
        in_specs.append(pl.BlockSpec(memory_space=pl.ANY))
        operands.append(w["w_in"])
        out_specs.append(pl.BlockSpec(memory_space=pl.ANY))
        out_shape.append(jax.ShapeDtypeStruct((D_MODEL, IN_W), BF16))
        scratch_shapes = [
            pltpu.VMEM((D_MODEL // n, IN_W), F32),
            pltpu.VMEM((D_MODEL // n, IN_W), BF16),
            pltpu.SemaphoreType.DMA((2,)),
        ]
    return pl.pallas_call(
        functools.partial(_output_kernel, final_norm=final_norm, next_layer=next_layer,
                          n_prompt_steps=n),
        grid=(n + 1,),
        in_specs=in_specs,
        out_specs=out_specs,
        out_shape=out_shape,
        scratch_shapes=scratch_shapes,
        compiler_params=pltpu.CompilerParams(
            dimension_semantics=("arbitrary",),
            vmem_limit_bytes=VMEM_LIMIT_BYTES),
        name=f"output_l{layer}",
    )(*operands)


def _trunks(x_p, p_p, x_s, p_s, hist0_s, s0_s, offset_s, w):
    B, L, _ = x_p.shape
    Bs, Ls, _ = x_s.shape
    rows_p, rows_s = B * L, Bs * Ls
    tables_p = _retention_tables(RET_CHUNK, RET_CHUNK)
    tables_s = _retention_tables(rows_s, Ls)
    rope_p = _rope_tables(jnp.arange(L))
    rope_s = _rope_tables(offset_s + jnp.arange(rows_s) % Ls)
    p2d_p = p_p.reshape(DEPTH, rows_p, PLE_DIM)
    p2d_s = p_s.reshape(DEPTH, rows_s, PLE_DIM)
    hist0_s = jnp.pad(hist0_s, ((0, 0), (0, 0), (HIST_ROWS - POOL_HIST, 0), (0, 0)))
    h_p, h_s = x_p, x_s.reshape(rows_s, D_MODEL)
    hists_p, states_p, hists_s, states_s = [], [], [], []
    win_b = w["w_in"][0].astype(BF16)
    for layer in range(DEPTH):
        mix_p, hist, state, wout_b, wpg_b = _mixer_prompt(h_p, layer, 0, tables_p, rope_p, w, win_b)
        hists_p.append(hist)
        states_p.append(state)
        mix_s, hist, state = _mixer_sample(h_s, hist0_s, s0_s, layer, offset_s, Ls, tables_s,
                                           rope_s, w, win_b)
        hists_s.append(hist)
        states_s.append(state)
        outs = _output_call(h_p.reshape(rows_p, D_MODEL), mix_p.reshape(rows_p, MIX_W), p2d_p,
                            h_s, mix_s, p2d_s, layer, w, wout_b, wpg_b)
        h_p, h_s = outs[0].reshape(B, L, D_MODEL), outs[1]
        if layer + 1 < DEPTH:
            win_b = outs[2]
    stack = lambda xs: jnp.stack(xs, axis=0)
    return (h_p, h_s.reshape(Bs, Ls, D_MODEL), stack(hists_p), stack(states_p), stack(hists_s),
            stack(states_s))


def kernel(x_prompt, x_sample, p_prompt, p_sample, state_pool, state_ret, g_mix, w_in, w_pool,
           pool_scale, g_gn, w_out, g_ple, w_pg, w_ple, g_final):
    past_len = 1024
    row = lambda a: a.reshape(a.shape[0], 1, a.shape[1])
    w = {
        "g_mix": row(g_mix), "w_in": w_in, "w_pool": w_pool.astype(BF16),
        "pool_scale": row(pool_scale), "g_gn": row(g_gn), "w_out": w_out,
        "g_ple": row(g_ple), "w_pg": w_pg, "w_ple": w_ple.astype(BF16),
        "g_final": g_final.reshape(1, D_MODEL),
    }
    return _trunks(x_prompt, p_prompt, x_sample, p_sample, state_pool, state_ret, past_len, w)
```

```python
import functools

import jax
import jax.numpy as jnp
import numpy as np
from jax import lax
from jax.experimental import pallas as pl
from jax.experimental.pallas import tpu as pltpu

D_MODEL = 2048
DEPTH = 4
PLE_DIM = 256
POOL_W = 1024
POOL_WINDOWS = (2, 4, 8, 16)
POOL_GW = POOL_W // len(POOL_WINDOWS)
POOL_HIST = max(POOL_WINDOWS) - 1
HIST_ROWS = POOL_HIST + 1
RET_HEADS = 4
RET_DK = 256
RET_DV = 256
RET_W = RET_HEADS * RET_DV
MIX_W = POOL_W + RET_W
IN_W = 2 * POOL_W + 2 * RET_HEADS * RET_DK + 2 * RET_W
ROPE_BASE = 10000.0
EPS = 1e-6
GN_EPS = 1e-5
HALF = RET_DK // 2

RET_CHUNK = 256
ROW_TILE = 512
OUT_ROW_TILE = 512
SEQS_PER_STEP = 2
VMEM_LIMIT_BYTES = 56 * 1024 * 1024

F32 = jnp.float32
BF16 = jnp.bfloat16


def _resident(block_shape, index_map):
    return pl.BlockSpec(block_shape, index_map, pipeline_mode=pl.Buffered(1))


def _rmsnorm(x, g):
    ms = jnp.mean(x * x, axis=-1, keepdims=True)
    return x * lax.rsqrt(ms + EPS) * g


def _silu(x):
    return x * jax.nn.sigmoid(x)


def _dot(a, b):
    return jnp.dot(a, b, preferred_element_type=F32)


def _dot_nt(a, b):
    return lax.dot_general(a, b, (((1,), (1,)), ((), ())), preferred_element_type=F32)


def _dot_tn(a, b):
    return lax.dot_general(a, b, (((0,), (0,)), ((), ())), preferred_element_type=F32)


def _rotate(x, cos, sin):
    x1, x2 = x[:, :HALF], x[:, HALF:]
    return jnp.concatenate([x1 * cos - x2 * sin, x1 * sin + x2 * cos], axis=-1)


def _both_halves(ref, hd):
    return jnp.concatenate([ref[hd], ref[hd]], axis=-1)


def _pool_inputs(ubuf_ref, pos, seq_rows):
    n_seq = ubuf_ref.shape[0]
    rows = n_seq * seq_rows
    pooled = []
    for g, w in enumerate(POOL_WINDOWS):
        cols = slice(g * POOL_GW, (g + 1) * POOL_GW)
        u_g = ubuf_ref[:, HIST_ROWS:HIST_ROWS + seq_rows, cols]
        acc = u_g
        for j in range(1, w):
            acc = acc + ubuf_ref[:, HIST_ROWS - j:HIST_ROWS - j + seq_rows, cols]
        acc = acc.reshape(rows, POOL_GW)
        u_g = u_g.reshape(rows, POOL_GW)
        cnt = jnp.minimum(pos + 1, w).astype(F32)
        pooled.append((acc / cnt - u_g).astype(BF16))
    return pooled


def _pool_outputs(pooled, gp, wpool_ref, pscale_ref, mix_ref):
    for g in range(len(POOL_WINDOWS)):
        cols = slice(g * POOL_GW, (g + 1) * POOL_GW)
        y = _dot(pooled[g], wpool_ref[g])
        y = y * pscale_ref[:, cols] * _silu(gp[:, cols])
        mix_ref[:, cols] = y.astype(mix_ref.dtype)


def _group_norm_gate(o, ggn, gate):
    mu = jnp.mean(o, axis=-1, keepdims=True)
    d = o - mu
    var = jnp.mean(d * d, axis=-1, keepdims=True)
    return d * lax.rsqrt(var + GN_EPS) * ggn * gate


class _SideConvert:
    def __init__(self, src_hbm, layer, dst_hbm, in_ref, out_ref, sems, step, n_steps):
        self.src, self.layer, self.dst = src_hbm, layer, dst_hbm
        self.in_ref, self.out_ref, self.sems = in_ref, out_ref, sems
        self.step, self.n_steps = step, n_steps
        self.rows = in_ref.shape[0]

    def _load(self, j):
        r0 = pl.multiple_of(j * self.rows, self.rows)
        return pltpu.make_async_copy(self.src.at[self.layer, pl.ds(r0, self.rows), :],
                                     self.in_ref, self.sems.at[0])

    def _store(self, j):
        r0 = pl.multiple_of(j * self.rows, self.rows)
        return pltpu.make_async_copy(self.out_ref, self.dst.at[pl.ds(r0, self.rows), :],
                                     self.sems.at[1])

    def head(self):
        @pl.when(self.step == 0)
        def _():
            self._load(0).start()

        @pl.when((self.step > 0) & (self.step < self.n_steps))
        def _():
            self._store(self.step - 1).wait()

        @pl.when(self.step < self.n_steps)
        def _():
            self._load(self.step).wait()
            self.out_ref[...] = self.in_ref[...].astype(self.out_ref.dtype)
            self._store(self.step).start()
            self._load(jnp.minimum(self.step + 1, self.n_steps - 1)).start()

    def tail(self):
        @pl.when(self.step == self.n_steps - 1)
        def _():
            self._store(self.step).wait()
            self._load(self.step).wait()


def _mixer_prompt_kernel(h_ref, cos_ref, sin_ref, dmat_ref, qdec_ref, kdec_ref, cdec_ref,
                         gmix_ref, win_ref, wpool_ref, pscale_ref, ggn_ref, wout_hbm, wpg_hbm,
                         states_hbm,
                         mix_ref, hist_ref, s_ref, wout_bf16_hbm, wpg_bf16_hbm,
                         ubuf_ref, wout_in, wout_out, wout_sems, wpg_in, wpg_out, wpg_sems,
                         *, offset, layer):
    del states_hbm
    T = h_ref.shape[0]
    C = dmat_ref.shape[1]
    t = pl.program_id(1)
    step = pl.program_id(0) * pl.num_programs(1) + t
    n_steps = pl.num_programs(0) * pl.num_programs(1)
    sides = [
        _SideConvert(wout_hbm, layer, wout_bf16_hbm, wout_in, wout_out, wout_sems, step, n_steps),
        _SideConvert(wpg_hbm, layer, wpg_bf16_hbm, wpg_in, wpg_out, wpg_sems, step, n_steps),
    ]
    for side in sides:
        side.head()

    @pl.when(t == 0)
    def _():
        ubuf_ref[:, :HIST_ROWS, :] = jnp.zeros((1, HIST_ROWS, POOL_W), F32)
        s_ref[...] = jnp.zeros(s_ref.shape, F32)

    for c in range(T // C):
        rows = slice(c * C, (c + 1) * C)
        hn = _rmsnorm(h_ref[rows, :], gmix_ref[...]).astype(BF16)

        def proj(j, hn=hn):
            return _dot(hn, win_ref[:, j * POOL_W:(j + 1) * POOL_W])

        ubuf_ref[0, HIST_ROWS + c * C:HIST_ROWS + (c + 1) * C, :] = proj(0)
        gp = proj(1)
        pos = offset + t * T + c * C + lax.broadcasted_iota(jnp.int32, (C, POOL_GW), 0)
        pooled = _pool_inputs(ubuf_ref.at[:, c * C:HIST_ROWS + (c + 1) * C, :], pos, C)
        q, k, v, gr = proj(2), proj(3), proj(4).astype(BF16), proj(5)
        _pool_outputs(pooled, gp, wpool_ref, pscale_ref, mix_ref.at[rows, :])

        cos, sin = cos_ref[rows, :], sin_ref[rows, :]
        head_cols = [slice(hd * RET_DK, (hd + 1) * RET_DK) for hd in range(RET_HEADS)]
        scores, cross = [], []
        for hd, cols in enumerate(head_cols):
            qr = _rotate(q[:, cols], cos, sin).astype(BF16)
            kr = _rotate(k[:, cols], cos, sin) * (RET_DK ** -0.5)
            kd = (kr * _both_halves(kdec_ref, hd)).astype(BF16)
            scores.append(_dot_nt(qr, kr.astype(BF16)))
            s_old = s_ref[hd]
            cross.append(_dot(qr, s_old.astype(BF16)) * _both_halves(qdec_ref, hd))
            s_ref[hd] = s_old * cdec_ref[hd] + _dot_tn(kd, v[:, cols])
        for hd, cols in enumerate(head_cols):
            p = (scores[hd] * dmat_ref[hd]).astype(BF16)
            o = _dot(p, v[:, cols]) + cross[hd]
            on = _group_norm_gate(o, ggn_ref[:, cols], _silu(gr[:, cols]))
            mix_ref[rows, POOL_W + hd * RET_DV:POOL_W + (hd + 1) * RET_DV] = on.astype(mix_ref.dtype)

    @pl.when(t == pl.num_programs(1) - 1)
    def _():
        hist_ref[...] = ubuf_ref[0, T + 1:T + HIST_ROWS, :]

    ubuf_ref[0, :HIST_ROWS, :] = ubuf_ref[0, T:T + HIST_ROWS, :]

    for side in sides:
        side.tail()


def _mixer_sample_kernel(h_ref, cos_ref, sin_ref, dmat_ref, qdec_ref, kdec_ref, cdec_ref,
                         hist0_ref, s0_ref, gmix_ref, win_ref, wpool_ref, pscale_ref, ggn_ref,
                         states_hbm,
                         mix_ref, hist_ref, s_ref,
                         ubuf_ref, qr_ref, kdt_ref, v_ref, oin_ref, ox_ref, gate_ref,
                         *, offset, seq_rows):
    del states_hbm
    rows = h_ref.shape[0]
    s = pl.program_id(0)

    @pl.when(s == 0)
    def _():
        n_seq = rows // seq_rows
        hn = _rmsnorm(h_ref[...], gmix_ref[...]).astype(BF16)

        def proj(j):
            return _dot(hn, win_ref[:, j * POOL_W:(j + 1) * POOL_W])

        ubuf_ref[:, :HIST_ROWS, :] = hist0_ref[...]
        ubuf_ref[:, HIST_ROWS:, :] = proj(0).reshape(n_seq, seq_rows, POOL_W)
        gp = proj(1)
        row = lax.broadcasted_iota(jnp.int32, (rows, POOL_GW), 0)
        pos = offset + lax.rem(row, seq_rows)
        pooled = _pool_inputs(ubuf_ref, pos, seq_rows)
        hist_ref[...] = ubuf_ref[:, seq_rows + 1:seq_rows + HIST_ROWS, :]
        q, k = proj(2), proj(3)
        _pool_outputs(pooled, gp, wpool_ref, pscale_ref, mix_ref)

        v_ref[...] = proj(4).astype(BF16)
        gate_ref[...] = _silu(proj(5))
        cos, sin = cos_ref[...], sin_ref[...]
        for hd in range(RET_HEADS):
            cols = slice(hd * RET_DK, (hd + 1) * RET_DK)
            qr = _rotate(q[:, cols], cos, sin).astype(BF16)
            kr = _rotate(k[:, cols], cos, sin) * (RET_DK ** -0.5)
            qr_ref[:, cols] = qr
            kdt_ref[hd] = (kr * _both_halves(kdec_ref, hd)).T.astype(BF16)
            p = (_dot_nt(qr, kr.astype(BF16)) * dmat_ref[hd]).astype(BF16)
            oin_ref[:, cols] = _dot(p, v_ref[:, cols])

    row_id = lax.broadcasted_iota(jnp.int32, (rows, RET_DV), 0)
    for i in range(s0_ref.shape[0]):
        r0 = pl.multiple_of((s * s0_ref.shape[0] + i) * seq_rows, seq_rows)
        in_seq = (row_id >= r0) & (row_id < r0 + seq_rows)
        for hd in range(RET_HEADS):
            cols = slice(hd * RET_DK, (hd + 1) * RET_DK)
            s_old = s0_ref[i, hd]
            ox_ref[pl.ds(r0, seq_rows), cols] = _dot(qr_ref[pl.ds(r0, seq_rows), cols],
                                                     s_old.astype(BF16))
            vh = v_ref[:, cols]
            v_seq = jnp.where(in_seq, vh, jnp.zeros_like(vh))
            s_ref[i, hd] = s_old * cdec_ref[hd] + _dot(kdt_ref[hd], v_seq)

    @pl.when(s == pl.num_programs(0) - 1)
    def _():
        for hd in range(RET_HEADS):
            cols = slice(hd * RET_DK, (hd + 1) * RET_DK)
            o = oin_ref[:, cols] + ox_ref[:, cols] * _both_halves(qdec_ref, hd)
            on = _group_norm_gate(o, ggn_ref[:, cols], gate_ref[:, cols])
            mix_ref[:, POOL_W + hd * RET_DV:POOL_W + (hd + 1) * RET_DV] = on.astype(mix_ref.dtype)


def _output_body(h_ref, mix_ref, p_ref, wout_ref, gple_ref, wpg_ref, wple_ref, gfin_ref,
                 out_ref, final_norm):
    h = h_ref[...] + _dot(mix_ref[...], wout_ref[...])
    hn = _rmsnorm(h, gple_ref[...]).astype(BF16)
    gate = jax.nn.sigmoid(_dot(hn, wpg_ref[...]))
    h = h + gate * _dot(p_ref[...].astype(BF16), wple_ref[...])
    if final_norm:
        h = _rmsnorm(h, gfin_ref[...])
    out_ref[...] = h


def _output_kernel(h_ref, mix_ref, p_ref, hs_ref, mixs_ref, ps_ref,
                   wout_ref, gple_ref, wpg_ref, wple_ref, gfin_ref, *rest,
                   final_norm, next_layer, n_prompt_steps):
    i = pl.program_id(0)
    side = None
    if next_layer is None:
        out_ref, outs_ref = rest
    else:
        win_next_hbm, out_ref, outs_ref, win_next_bf16_hbm, win_in, win_out, side_sems = rest
        side = _SideConvert(win_next_hbm, next_layer, win_next_bf16_hbm, win_in, win_out,
                            side_sems, i, n_prompt_steps)
        side.head()

    @pl.when(i == n_prompt_steps)
    def _():
        _output_body(hs_ref, mixs_ref, ps_ref, wout_ref, gple_ref, wpg_ref, wple_ref, gfin_ref,
                     outs_ref, final_norm)

    @pl.when(i != n_prompt_steps)
    def _():
        _output_body(h_ref, mix_ref, p_ref, wout_ref, gple_ref, wpg_ref, wple_ref, gfin_ref,
                     out_ref, final_norm)

    if side is not None:
        side.tail()


def _retention_tables(rows, seq_rows):
    lg = np.log(1.0 - 2.0 ** (-5.0 - np.arange(RET_HEADS, dtype=np.float64)))
    n = np.arange(rows)
    r = (n % seq_rows).astype(np.float64)
    same = (n // seq_rows)[:, None] == (n // seq_rows)[None, :]
    diff = r[:, None] - r[None, :]
    keep = same & (diff >= 0)
    dmat = np.where(keep[None], np.exp(np.maximum(diff, 0.0)[None] * lg[:, None, None]), 0.0)
    qdec = np.exp((r[None, :] + 1.0) * lg[:, None])
    kdec = np.exp((seq_rows - 1.0 - r[None, :]) * lg[:, None])
    cdec = np.exp(seq_rows * lg)
    bcast = lambda a: np.broadcast_to(a[:, :, None], (RET_HEADS, rows, HALF))
    return tuple(jnp.asarray(a, F32) for a in (dmat, bcast(qdec), bcast(kdec), cdec))


def _rope_tables(pos):
    inv = 1.0 / (ROPE_BASE ** (jnp.arange(HALF, dtype=F32) / HALF))
    ang = pos.astype(F32)[:, None] * inv[None, :]
    return jnp.cos(ang), jnp.sin(ang)


def _mixer_prompt(h, layer, offset, tables, rope, w, win_bf16, states):
    B, L, _ = h.shape
    T = ROW_TILE
    n_steps = B * (L // T)
    dmat, qdec, kdec, cdec = tables
    cos, sin = rope
    const = lambda *idx: (lambda b, t: idx)
    in_specs = [
        pl.BlockSpec((None, T, D_MODEL), lambda b, t: (b, t, 0)),
        pl.BlockSpec((T, HALF), lambda b, t: (t, 0)),
        pl.BlockSpec((T, HALF), lambda b, t: (t, 0)),
        _resident((RET_HEADS, RET_CHUNK, RET_CHUNK), const(0, 0, 0)),
        _resident((RET_HEADS, RET_CHUNK, HALF), const(0, 0, 0)),
        _resident((RET_HEADS, RET_CHUNK, HALF), const(0, 0, 0)),
        pl.BlockSpec(memory_space=pltpu.SMEM),
        _resident((None, 1, D_MODEL), const(layer, 0, 0)),
        _resident((D_MODEL, IN_W), const(0, 0)),
        _resident((None, len(POOL_WINDOWS), POOL_GW, POOL_GW), const(layer, 0, 0, 0)),
        _resident((None, 1, POOL_W), const(layer, 0, 0)),
        _resident((None, 1, RET_W), const(layer, 0, 0)),
        pl.BlockSpec(memory_space=pl.ANY),
        pl.BlockSpec(memory_space=pl.ANY),
        pl.BlockSpec(memory_space=pl.ANY),
    ]
    out_specs = [
        pl.BlockSpec((None, T, MIX_W), lambda b, t: (b, t, 0)),
        pl.BlockSpec((None, POOL_HIST, POOL_W), lambda b, t: (b, 0, 0)),
        pl.BlockSpec((None, None, RET_HEADS, RET_DK, RET_DV), lambda b, t: (layer, b, 0, 0, 0)),
        pl.BlockSpec(memory_space=pl.ANY),
        pl.BlockSpec(memory_space=pl.ANY),
    ]
    out_shape = [
        jax.ShapeDtypeStruct((B, L, MIX_W), BF16),
        jax.ShapeDtypeStruct((B, POOL_HIST, POOL_W), F32),
        jax.ShapeDtypeStruct(states.shape, F32),
        jax.ShapeDtypeStruct((MIX_W, D_MODEL), BF16),
        jax.ShapeDtypeStruct((D_MODEL, D_MODEL), BF16),
    ]
    scratch_shapes = [
        pltpu.VMEM((1, HIST_ROWS + T, POOL_W), F32),
        pltpu.VMEM((MIX_W // n_steps, D_MODEL), F32),
        pltpu.VMEM((MIX_W // n_steps, D_MODEL), BF16),
        pltpu.SemaphoreType.DMA((2,)),
        pltpu.VMEM((D_MODEL // n_steps, D_MODEL), F32),
        pltpu.VMEM((D_MODEL // n_steps, D_MODEL), BF16),
        pltpu.SemaphoreType.DMA((2,)),
    ]
    return pl.pallas_call(
        functools.partial(_mixer_prompt_kernel, offset=offset, layer=layer),
        grid=(B, L // T),
        in_specs=in_specs,
        out_specs=out_specs,
        out_shape=out_shape,
        scratch_shapes=scratch_shapes,
        compiler_params=pltpu.CompilerParams(
            dimension_semantics=("arbitrary", "arbitrary"),
            vmem_limit_bytes=VMEM_LIMIT_BYTES),
        input_output_aliases={len(in_specs) - 1: 2},
        name=f"mixer_prompt_l{layer}",
    )(h, cos, sin, dmat, qdec, kdec, cdec, w["g_mix"], win_bf16, w["w_pool"],
      w["pool_scale"], w["g_gn"], w["w_out"], w["w_pg"], states)


def _mixer_sample(h2d, hist0, s0, layer, offset, seq_rows, tables, rope, w, win_bf16, states):
    rows = h2d.shape[0]
    n_seq = rows // seq_rows
    dmat, qdec, kdec, cdec = tables
    cos, sin = rope
    const = lambda *idx: (lambda s: idx)
    in_specs = [
        _resident((rows, D_MODEL), const(0, 0)),
        _resident((rows, HALF), const(0, 0)),
        _resident((rows, HALF), const(0, 0)),
        _resident((RET_HEADS, rows, rows), const(0, 0, 0)),
        _resident((RET_HEADS, rows, HALF), const(0, 0, 0)),
        _resident((RET_HEADS, rows, HALF), const(0, 0, 0)),
        pl.BlockSpec(memory_space=pltpu.SMEM),
        _resident((None, n_seq, HIST_ROWS, POOL_W), const(layer, 0, 0, 0)),
        pl.BlockSpec((None, SEQS_PER_STEP, RET_HEADS, RET_DK, RET_DV),
                     lambda s: (layer, s, 0, 0, 0)),
        _resident((None, 1, D_MODEL), const(layer, 0, 0)),
        _resident((D_MODEL, IN_W), const(0, 0)),
        _resident((None, len(POOL_WINDOWS), POOL_GW, POOL_GW), const(layer, 0, 0, 0)),
        _resident((None, 1, POOL_W), const(layer, 0, 0)),
        _resident((None, 1, RET_W), const(layer, 0, 0)),
        pl.BlockSpec(memory_space=pl.ANY),
    ]
    out_specs = [
        pl.BlockSpec((rows, MIX_W), const(0, 0)),
        pl.BlockSpec((n_seq, POOL_HIST, POOL_W), const(0, 0, 0)),
        pl.BlockSpec((None, SEQS_PER_STEP, RET_HEADS, RET_DK, RET_DV),
                     lambda s: (layer, s, 0, 0, 0)),
    ]
    out_shape = [
        jax.ShapeDtypeStruct((rows, MIX_W), BF16),
        jax.ShapeDtypeStruct((n_seq, POOL_HIST, POOL_W), F32),
        jax.ShapeDtypeStruct(states.shape, F32),
    ]
    scratch_shapes = [
        pltpu.VMEM((n_seq, HIST_ROWS + seq_rows, POOL_W), F32),
        pltpu.VMEM((rows, RET_W), BF16),
        pltpu.VMEM((RET_HEADS, RET_DK, rows), BF16),
        pltpu.VMEM((rows, RET_W), BF16),
        pltpu.VMEM((rows, RET_W), F32),
        pltpu.VMEM((rows, RET_W), F32),
        pltpu.VMEM((rows, RET_W), F32),
    ]
    return pl.pallas_call(
        functools.partial(_mixer_sample_kernel, offset=offset, seq_rows=seq_rows),
        grid=(n_seq // SEQS_PER_STEP,),
        in_specs=in_specs,
        out_specs=out_specs,
        out_shape=out_shape,
        scratch_shapes=scratch_shapes,
        compiler_params=pltpu.CompilerParams(
            dimension_semantics=("arbitrary",),
            vmem_limit_bytes=VMEM_LIMIT_BYTES),
        input_output_aliases={len(in_specs) - 1: 2},
        name=f"mixer_sample_l{layer}",
    )(h2d, cos, sin, dmat, qdec, kdec, cdec, hist0, s0, w["g_mix"], win_bf16, w["w_pool"],
      w["pool_scale"], w["g_gn"], states)


def _output_call(h_p, mix_p, p2d_p, h_s, mix_s, p2d_s, layer, w, wout_bf16, wpg_bf16):
    rows_p, rows_s = h_p.shape[0], h_s.shape[0]
    T = min(OUT_ROW_TILE, rows_p)
    n = rows_p // T
    final_norm = layer == DEPTH - 1
    next_layer = None if final_norm else layer + 1
    const = lambda *idx: (lambda i: idx)
    tile = lambda i: jnp.minimum(i, n - 1)
    in_specs = [
        pl.BlockSpec((T, D_MODEL), lambda i: (tile(i), 0)),
        pl.BlockSpec((T, MIX_W), lambda i: (tile(i), 0)),
        pl.BlockSpec((None, T, PLE_DIM), lambda i: (layer, tile(i), 0)),
        _resident((rows_s, D_MODEL), const(0, 0)),
        _resident((rows_s, MIX_W), const(0, 0)),
        _resident((None, rows_s, PLE_DIM), const(layer, 0, 0)),
        _resident((MIX_W, D_MODEL), const(0, 0)),
        _resident((None, 1, D_MODEL), const(layer, 0, 0)),
        _resident((D_MODEL, D_MODEL), const(0, 0)),
        _resident((None, PLE_DIM, D_MODEL), const(layer, 0, 0)),
        _resident((1, D_MODEL), const(0, 0)),
    ]
    out_specs = [pl.BlockSpec((T, D_MODEL), lambda i: (tile(i), 0)),
                 pl.BlockSpec((rows_s, D_MODEL), const(0, 0))]
    out_shape = [jax.ShapeDtypeStruct((rows_p, D_MODEL), F32),
                 jax.ShapeDtypeStruct((rows_s, D_MODEL), F32)]
    operands = [h_p, mix_p, p2d_p, h_s, mix_s, p2d_s,
                wout_bf16, w["g_ple"], wpg_bf16, w["w_ple"], w["g_final"]]
    scratch_shapes = []
    if next_layer is not None:
        in_specs.append(pl.BlockSpec(memory_space=pl.ANY))
        operands.append(w["w_in"])
        out_specs.append(pl.BlockSpec(memory_space=pl.ANY))
        out_shape.append(jax.ShapeDtypeStruct((D_MODEL, IN_W), BF16))
        scratch_shapes = [
            pltpu.VMEM((D_MODEL // n, IN_W), F32),
            pltpu.VMEM((D_MODEL // n, IN_W), BF16),
            pltpu.SemaphoreType.DMA((2,)),
        ]
    return pl.pallas_call(
        functools.partial(_output_kernel, final_norm=final_norm, next_layer=next_layer,
                          n_prompt_steps=n),
        grid=(n + 1,),
        in_specs=in_specs,
        out_specs=out_specs,
        out_shape=out_shape,
        scratch_shapes=scratch_shapes,
        compiler_params=pltpu.CompilerParams(
            dimension_semantics=("arbitrary",),
            vmem_limit_bytes=VMEM_LIMIT_BYTES),
        name=f"output_l{layer}",
    )(*operands)


def _trunks(x_p, p_p, x_s, p_s, hist0_s, s0_s, offset_s, w):
    B, L, _ = x_p.shape
    Bs, Ls, _ = x_s.shape
    rows_p, rows_s = B * L, Bs * Ls
    tables_p = _retention_tables(RET_CHUNK, RET_CHUNK)
    tables_s = _retention_tables(rows_s, Ls)
    rope_p = _rope_tables(jnp.arange(L))
    rope_s = _rope_tables(offset_s + jnp.arange(rows_s) % Ls)
    p2d_p = p_p.reshape(DEPTH, rows_p, PLE_DIM)
    p2d_s = p_s.reshape(DEPTH, rows_s, PLE_DIM)
    hist0_s = jnp.pad(hist0_s, ((0, 0), (0, 0), (HIST_ROWS - POOL_HIST, 0), (0, 0)))
    h_p, h_s = x_p, x_s.reshape(rows_s, D_MODEL)
    hists_p, hists_s = [], []
    states_p = jnp.zeros((DEPTH, B, RET_HEADS, RET_DK, RET_DV), F32)
    states_s = jnp.zeros((DEPTH, Bs, RET_HEADS, RET_DK, RET_DV), F32)
    win_b = w["w_in"][0].astype(BF16)
    for layer in range(DEPTH):
        mix_p, hist, states_p, wout_b, wpg_b = _mixer_prompt(h_p, layer, 0, tables_p, rope_p, w,
                                                             win_b, states_p)
        hists_p.append(hist)
        mix_s, hist, states_s = _mixer_sample(h_s, hist0_s, s0_s, layer, offset_s, Ls, tables_s,
                                              rope_s, w, win_b, states_s)
        hists_s.append(hist)
        outs = _output_call(h_p.reshape(rows_p, D_MODEL), mix_p.reshape(rows_p, MIX_W), p2d_p,
                            h_s, mix_s, p2d_s, layer, w, wout_b, wpg_b)
        h_p, h_s = outs[0].reshape(B, L, D_MODEL), outs[1]
        if layer + 1 < DEPTH:
            win_b = outs[2]
    stack = lambda xs: jnp.stack(xs, axis=0)
    return (h_p, h_s.reshape(Bs, Ls, D_MODEL), stack(hists_p), states_p, stack(hists_s), states_s)


def kernel(x_prompt, x_sample, p_prompt, p_sample, state_pool, state_ret, g_mix, w_in, w_pool,
           pool_scale, g_gn, w_out, g_ple, w_pg, w_ple, g_final):
    past_len = 1024
    row = lambda a: a.reshape(a.shape[0], 1, a.shape[1])
    w = {
        "g_mix": row(g_mix), "w_in": w_in, "w_pool": w_pool.astype(BF16),
        "pool_scale": row(pool_scale), "g_gn": row(g_gn), "w_out": w_out,
        "g_ple": row(g_ple), "w_pg": w_pg, "w_ple": w_ple.astype(BF16),
        "g_final": g_final.reshape(1, D_MODEL),
    }
    return _trunks(x_prompt, p_prompt, x_sample, p_sample, state_pool, state_ret, past_len, w)
```

```python
import functools

import jax
import jax.numpy as jnp
import numpy as np
from jax import lax
from jax.experimental import pallas as pl
from jax.experimental.pallas import tpu as pltpu

D_MODEL = 2048
DEPTH = 4
PLE_DIM = 256
POOL_W = 1024
POOL_WINDOWS = (2, 4, 8, 16)
POOL_GW = POOL_W // len(POOL_WINDOWS)
POOL_HIST = max(POOL_WINDOWS) - 1
HIST_ROWS = POOL_HIST + 1
RET_HEADS = 4
RET_DK = 256
RET_DV = 256
RET_W = RET_HEADS * RET_DV
MIX_W = POOL_W + RET_W
IN_W = 2 * POOL_W + 2 * RET_HEADS * RET_DK + 2 * RET_W
ROPE_BASE = 10000.0
EPS = 1e-6
GN_EPS = 1e-5
HALF = RET_DK // 2

RET_CHUNK = 256
ROW_TILE = 512
OUT_ROW_TILE = 512
SEQS_PER_STEP = 2
VMEM_LIMIT_BYTES = 56 * 1024 * 1024

F32 = jnp.float32
BF16 = jnp.bfloat16


def _resident(block_shape, index_map):
    return pl.BlockSpec(block_shape, index_map, pipeline_mode=pl.Buffered(1))


def _rmsnorm(x, g):
    ms = jnp.mean(x * x, axis=-1, keepdims=True)
    return x * lax.rsqrt(ms + EPS) * g


def _silu(x):
    return x * jax.nn.sigmoid(x)


def _dot(a, b):
    return jnp.dot(a, b, preferred_element_type=F32)


def _dot_nt(a, b):
    return lax.dot_general(a, b, (((1,), (1,)), ((), ())), preferred_element_type=F32)


def _dot_tn(a, b):
    return lax.dot_general(a, b, (((0,), (0,)), ((), ())), preferred_element_type=F32)


def _rotate(x, cos, sin):
    x1, x2 = x[:, :HALF], x[:, HALF:]
    return jnp.concatenate([x1 * cos - x2 * sin, x1 * sin + x2 * cos], axis=-1)


def _both_halves(ref, hd):
    return jnp.concatenate([ref[hd], ref[hd]], axis=-1)


def _pool_inputs(ubuf_ref, pos, seq_rows):
    n_seq = ubuf_ref.shape[0]
    rows = n_seq * seq_rows
    pooled = []
    for g, w in enumerate(POOL_WINDOWS):
        cols = slice(g * POOL_GW, (g + 1) * POOL_GW)
        u_g = ubuf_ref[:, HIST_ROWS:HIST_ROWS + seq_rows, cols]
        acc = u_g
        for j in range(1, w):
            acc = acc + ubuf_ref[:, HIST_ROWS - j:HIST_ROWS - j + seq_rows, cols]
        acc = acc.reshape(rows, POOL_GW)
        u_g = u_g.reshape(rows, POOL_GW)
        cnt = jnp.minimum(pos + 1, w).astype(F32)
        pooled.append((acc / cnt - u_g).astype(BF16))
    return pooled


def _pool_outputs(pooled, gp, wpool_ref, pscale_ref, mix_ref):
    for g in range(len(POOL_WINDOWS)):
        cols = slice(g * POOL_GW, (g + 1) * POOL_GW)
        y = _dot(pooled[g], wpool_ref[g])
        y = y * pscale_ref[:, cols] * _silu(gp[:, cols])
        mix_ref[:, cols] = y.astype(mix_ref.dtype)


def _group_norm_gate(o, ggn, gate):
    mu = jnp.mean(o, axis=-1, keepdims=True)
    d = o - mu
    var = jnp.mean(d * d, axis=-1, keepdims=True)
    return d * lax.rsqrt(var + GN_EPS) * ggn * gate


class _SideConvert:
    def __init__(self, src_hbm, layer, dst_hbm, in_ref, out_ref, sems, step, n_steps):
        self.src, self.layer, self.dst = src_hbm, layer, dst_hbm
        self.in_ref, self.out_ref, self.sems = in_ref, out_ref, sems
        self.step, self.n_steps = step, n_steps
        self.rows = in_ref.shape[0]

    def _load(self, j):
        r0 = pl.multiple_of(j * self.rows, self.rows)
        return pltpu.make_async_copy(self.src.at[self.layer, pl.ds(r0, self.rows), :],
                                     self.in_ref, self.sems.at[0])

    def _store(self, j):
        r0 = pl.multiple_of(j * self.rows, self.rows)
        return pltpu.make_async_copy(self.out_ref, self.dst.at[pl.ds(r0, self.rows), :],
                                     self.sems.at[1])

    def head(self):
        @pl.when(self.step == 0)
        def _():
            self._load(0).start()

        @pl.when((self.step > 0) & (self.step < self.n_steps))
        def _():
            self._store(self.step - 1).wait()

        @pl.when(self.step < self.n_steps)
        def _():
            self._load(self.step).wait()
            self.out_ref[...] = self.in_ref[...].astype(self.out_ref.dtype)
            self._store(self.step).start()
            self._load(jnp.minimum(self.step + 1, self.n_steps - 1)).start()

    def tail(self):
        @pl.when(self.step == self.n_steps - 1)
        def _():
            self._store(self.step).wait()
            self._load(self.step).wait()


def _mixer_prompt_kernel(h_ref, cos_ref, sin_ref, dmat_ref, qdec_ref, kdec_ref, cdec_ref,
                         gmix_ref, win_ref, wpool_ref, pscale_ref, ggn_ref, wout_hbm, wpg_hbm,
                         states_hbm,
                         mix_ref, hist_ref, s_ref, wout_bf16_hbm, wpg_bf16_hbm,
                         ubuf_ref, wout_in, wout_out, wout_sems, wpg_in, wpg_out, wpg_sems,
                         *, offset, layer):
    del states_hbm
    T = h_ref.shape[0]
    C = dmat_ref.shape[1]
    t = pl.program_id(1)
    step = pl.program_id(0) * pl.num_programs(1) + t
    n_steps = pl.num_programs(0) * pl.num_programs(1)
    sides = [
        _SideConvert(wout_hbm, layer, wout_bf16_hbm, wout_in, wout_out, wout_sems, step, n_steps),
        _SideConvert(wpg_hbm, layer, wpg_bf16_hbm, wpg_in, wpg_out, wpg_sems, step, n_steps),
    ]
    for side in sides:
        side.head()

    @pl.when(t == 0)
    def _():
        ubuf_ref[:, :HIST_ROWS, :] = jnp.zeros((1, HIST_ROWS, POOL_W), F32)
        s_ref[...] = jnp.zeros(s_ref.shape, F32)

    for c in range(T // C):
        rows = slice(c * C, (c + 1) * C)
        hn = _rmsnorm(h_ref[rows, :], gmix_ref[...]).astype(BF16)

        def proj(j, hn=hn):
            return _dot(hn, win_ref[:, j * POOL_W:(j + 1) * POOL_W])

        ubuf_ref[0, HIST_ROWS + c * C:HIST_ROWS + (c + 1) * C, :] = proj(0)
        gp = proj(1)
        pos = offset + t * T + c * C + lax.broadcasted_iota(jnp.int32, (C, POOL_GW), 0)
        pooled = _pool_inputs(ubuf_ref.at[:, c * C:HIST_ROWS + (c + 1) * C, :], pos, C)
        q, k, v, gr = proj(2), proj(3), proj(4).astype(BF16), proj(5)
        _pool_outputs(pooled, gp, wpool_ref, pscale_ref, mix_ref.at[rows, :])

        cos, sin = cos_ref[rows, :], sin_ref[rows, :]
        head_cols = [slice(hd * RET_DK, (hd + 1) * RET_DK) for hd in range(RET_HEADS)]
        scores, cross = [], []
        for hd, cols in enumerate(head_cols):
            qr = _rotate(q[:, cols], cos, sin).astype(BF16)
            kr = _rotate(k[:, cols], cos, sin) * (RET_DK ** -0.5)
            kd = (kr * _both_halves(kdec_ref, hd)).astype(BF16)
            scores.append(_dot_nt(qr, kr.astype(BF16)))
            s_old = s_ref[hd]
            cross.append(_dot(qr, s_old.astype(BF16)) * _both_halves(qdec_ref, hd))
            s_ref[hd] = s_old * cdec_ref[hd] + _dot_tn(kd, v[:, cols])
        for hd, cols in enumerate(head_cols):
            p = (scores[hd] * dmat_ref[hd]).astype(BF16)
            o = _dot(p, v[:, cols]) + cross[hd]
            on = _group_norm_gate(o, ggn_ref[:, cols], _silu(gr[:, cols]))
            mix_ref[rows, POOL_W + hd * RET_DV:POOL_W + (hd + 1) * RET_DV] = on.astype(mix_ref.dtype)

    @pl.when(t == pl.num_programs(1) - 1)
    def _():
        hist_ref[...] = ubuf_ref[0, T + 1:T + HIST_ROWS, :]

    ubuf_ref[0, :HIST_ROWS, :] = ubuf_ref[0, T:T + HIST_ROWS, :]

    for side in sides:
        side.tail()


def _mixer_sample_kernel(h_ref, cos_ref, sin_ref, dmat_ref, qdec_ref, kdec_ref, cdec_ref,
                         hist0_ref, s0_ref, gmix_ref, win_ref, wpool_ref, pscale_ref, ggn_ref,
                         states_hbm,
                         mix_ref, hist_ref, s_ref,
                         ubuf_ref, qr_ref, kdt_ref, v_ref, oin_ref, ox_ref, gate_ref,
                         *, offset, seq_rows):
    del states_hbm
    rows = h_ref.shape[0]
    s = pl.program_id(0)

    @pl.when(s == 0)
    def _():
        n_seq = rows // seq_rows
        hn = _rmsnorm(h_ref[...], gmix_ref[...]).astype(BF16)

        def proj(j):
            return _dot(hn, win_ref[:, j * POOL_W:(j + 1) * POOL_W])

        ubuf_ref[:, :HIST_ROWS, :] = hist0_ref[...]
        ubuf_ref[:, HIST_ROWS:, :] = proj(0).reshape(n_seq, seq_rows, POOL_W)
        gp = proj(1)
        row = lax.broadcasted_iota(jnp.int32, (rows, POOL_GW), 0)
        pos = offset + lax.rem(row, seq_rows)
        pooled = _pool_inputs(ubuf_ref, pos, seq_rows)
        hist_ref[...] = ubuf_ref[:, seq_rows + 1:seq_rows + HIST_ROWS, :]
        q, k = proj(2), proj(3)
        _pool_outputs(pooled, gp, wpool_ref, pscale_ref, mix_ref)

        v_ref[...] = proj(4).astype(BF16)
        gate_ref[...] = _silu(proj(5))
        cos, sin = cos_ref[...], sin_ref[...]
        for hd in range(RET_HEADS):
            cols = slice(hd * RET_DK, (hd + 1) * RET_DK)
            qr = _rotate(q[:, cols], cos, sin).astype(BF16)
            kr = _rotate(k[:, cols], cos, sin) * (RET_DK ** -0.5)
            qr_ref[:, cols] = qr
            kdt_ref[hd] = (kr * _both_halves(kdec_ref, hd)).T.astype(BF16)
            p = (_dot_nt(qr, kr.astype(BF16)) * dmat_ref[hd]).astype(BF16)
            oin_ref[:, cols] = _dot(p, v_ref[:, cols])

    row_id = lax.broadcasted_iota(jnp.int32, (rows, RET_DV), 0)
    for i in range(s0_ref.shape[0]):
        r0 = pl.multiple_of((s * s0_ref.shape[0] + i) * seq_rows, seq_rows)
        in_seq = (row_id >= r0) & (row_id < r0 + seq_rows)
        for hd in range(RET_HEADS):
            cols = slice(hd * RET_DK, (hd + 1) * RET_DK)
            s_old = s0_ref[i, hd]
            ox_ref[pl.ds(r0, seq_rows), cols] = _dot(qr_ref[pl.ds(r0, seq_rows), cols],
                                                     s_old.astype(BF16))
            vh = v_ref[:, cols]
            v_seq = jnp.where(in_seq, vh, jnp.zeros_like(vh))
            s_ref[i, hd] = s_old * cdec_ref[hd] + _dot(kdt_ref[hd], v_seq)

    @pl.when(s == pl.num_programs(0) - 1)
    def _():
        for hd in range(RET_HEADS):
            cols = slice(hd * RET_DK, (hd + 1) * RET_DK)
            o = oin_ref[:, cols] + ox_ref[:, cols] * _both_halves(qdec_ref, hd)
            on = _group_norm_gate(o, ggn_ref[:, cols], gate_ref[:, cols])
            mix_ref[:, POOL_W + hd * RET_DV:POOL_W + (hd + 1) * RET_DV] = on.astype(mix_ref.dtype)


def _output_body(h_ref, mix_ref, p_ref, wout_ref, gple_ref, wpg_ref, wple_ref, gfin_ref,
                 out_ref, final_norm):
    h = h_ref[...] + _dot(mix_ref[...], wout_ref[...])
    hn = _rmsnorm(h, gple_ref[...]).astype(BF16)
    gate = jax.nn.sigmoid(_dot(hn, wpg_ref[...]))
    h = h + gate * _dot(p_ref[...].astype(BF16), wple_ref[...])
    if final_norm:
        h = _rmsnorm(h, gfin_ref[...])
    out_ref[...] = h


def _output_kernel(h_ref, mix_ref, p_ref, hs_ref, mixs_ref, ps_ref,
                   wout_ref, gple_ref, wpg_ref, wple_ref, gfin_ref, *rest,
                   final_norm, next_layer, n_prompt_steps):
    i = pl.program_id(0)
    side = None
    if next_layer is None:
        out_ref, outs_ref = rest
    else:
        win_next_hbm, out_ref, outs_ref, win_next_bf16_hbm, win_in, win_out, side_sems = rest
        side = _SideConvert(win_next_hbm, next_layer, win_next_bf16_hbm, win_in, win_out,
                            side_sems, i, n_prompt_steps)
        side.head()

    @pl.when(i == n_prompt_steps)
    def _():
        _output_body(hs_ref, mixs_ref, ps_ref, wout_ref, gple_ref, wpg_ref, wple_ref, gfin_ref,
                     outs_ref, final_norm)

    @pl.when(i != n_prompt_steps)
    def _():
        _output_body(h_ref, mix_ref, p_ref, wout_ref, gple_ref, wpg_ref, wple_ref, gfin_ref,
                     out_ref, final_norm)

    if side is not None:
        side.tail()


def _retention_tables(rows, seq_rows):
    lg = np.log(1.0 - 2.0 ** (-5.0 - np.arange(RET_HEADS, dtype=np.float64)))
    n = np.arange(rows)
    r = (n % seq_rows).astype(np.float64)
    same = (n // seq_rows)[:, None] == (n // seq_rows)[None, :]
    diff = r[:, None] - r[None, :]
    keep = same & (diff >= 0)
    dmat = np.where(keep[None], np.exp(np.maximum(diff, 0.0)[None] * lg[:, None, None]), 0.0)
    qdec = np.exp((r[None, :] + 1.0) * lg[:, None])
    kdec = np.exp((seq_rows - 1.0 - r[None, :]) * lg[:, None])
    cdec = np.exp(seq_rows * lg)
    bcast = lambda a: np.broadcast_to(a[:, :, None], (RET_HEADS, rows, HALF))
    return tuple(jnp.asarray(a, F32) for a in (dmat, bcast(qdec), bcast(kdec), cdec))


def _rope_tables(pos):
    inv = 1.0 / (ROPE_BASE ** (np.arange(HALF, dtype=np.float64) / HALF))
    ang = pos.astype(np.float64)[:, None] * inv[None, :]
    return jnp.asarray(np.cos(ang), F32), jnp.asarray(np.sin(ang), F32)


def _mixer_prompt(h, layer, offset, tables, rope, w, win_bf16, states):
    B, L, _ = h.shape
    T = ROW_TILE
    n_steps = B * (L // T)
    dmat, qdec, kdec, cdec = tables
    cos, sin = rope
    const = lambda *idx: (lambda b, t: idx)
    in_specs = [
        pl.BlockSpec((None, T, D_MODEL), lambda b, t: (b, t, 0)),
        pl.BlockSpec((T, HALF), lambda b, t: (t, 0)),
        pl.BlockSpec((T, HALF), lambda b, t: (t, 0)),
        _resident((RET_HEADS, RET_CHUNK, RET_CHUNK), const(0, 0, 0)),
        _resident((RET_HEADS, RET_CHUNK, HALF), const(0, 0, 0)),
        _resident((RET_HEADS, RET_CHUNK, HALF), const(0, 0, 0)),
        pl.BlockSpec(memory_space=pltpu.SMEM),
        _resident((None, 1, D_MODEL), const(layer, 0, 0)),
        _resident((D_MODEL, IN_W), const(0, 0)),
        _resident((None, len(POOL_WINDOWS), POOL_GW, POOL_GW), const(layer, 0, 0, 0)),
        _resident((None, 1, POOL_W), const(layer, 0, 0)),
        _resident((None, 1, RET_W), const(layer, 0, 0)),
        pl.BlockSpec(memory_space=pl.ANY),
        pl.BlockSpec(memory_space=pl.ANY),
        pl.BlockSpec(memory_space=pl.ANY),
    ]
    out_specs = [
        pl.BlockSpec((None, T, MIX_W), lambda b, t: (b, t, 0)),
        pl.BlockSpec((None, POOL_HIST, POOL_W), lambda b, t: (b, 0, 0)),
        pl.BlockSpec((None, None, RET_HEADS, RET_DK, RET_DV), lambda b, t: (layer, b, 0, 0, 0)),
        pl.BlockSpec(memory_space=pl.ANY),
        pl.BlockSpec(memory_space=pl.ANY),
    ]
    out_shape = [
        jax.ShapeDtypeStruct((B, L, MIX_W), BF16),
        jax.ShapeDtypeStruct((B, POOL_HIST, POOL_W), F32),
        jax.ShapeDtypeStruct(states.shape, F32),
        jax.ShapeDtypeStruct((MIX_W, D_MODEL), BF16),
        jax.ShapeDtypeStruct((D_MODEL, D_MODEL), BF16),
    ]
    scratch_shapes = [
        pltpu.VMEM((1, HIST_ROWS + T, POOL_W), F32),
        pltpu.VMEM((MIX_W // n_steps, D_MODEL), F32),
        pltpu.VMEM((MIX_W // n_steps, D_MODEL), BF16),
        pltpu.SemaphoreType.DMA((2,)),
        pltpu.VMEM((D_MODEL // n_steps, D_MODEL), F32),
        pltpu.VMEM((D_MODEL // n_steps, D_MODEL), BF16),
        pltpu.SemaphoreType.DMA((2,)),
    ]
    return pl.pallas_call(
        functools.partial(_mixer_prompt_kernel, offset=offset, layer=layer),
        grid=(B, L // T),
        in_specs=in_specs,
        out_specs=out_specs,
        out_shape=out_shape,
        scratch_shapes=scratch_shapes,
        compiler_params=pltpu.CompilerParams(
            dimension_semantics=("arbitrary", "arbitrary"),
            vmem_limit_bytes=VMEM_LIMIT_BYTES),
        input_output_aliases={len(in_specs) - 1: 2},
        name=f"mixer_prompt_l{layer}",
    )(h, cos, sin, dmat, qdec, kdec, cdec, w["g_mix"], win_bf16, w["w_pool"],
      w["pool_scale"], w["g_gn"], w["w_out"], w["w_pg"], states)


def _mixer_sample(h2d, hist0, s0, layer, offset, seq_rows, tables, rope, w, win_bf16, states):
    rows = h2d.shape[0]
    n_seq = rows // seq_rows
    dmat, qdec, kdec, cdec = tables
    cos, sin = rope
    const = lambda *idx: (lambda s: idx)
    in_specs = [
        _resident((rows, D_MODEL), const(0, 0)),
        _resident((rows, HALF), const(0, 0)),
        _resident((rows, HALF), const(0, 0)),
        _resident((RET_HEADS, rows, rows), const(0, 0, 0)),
        _resident((RET_HEADS, rows, HALF), const(0, 0, 0)),
        _resident((RET_HEADS, rows, HALF), const(0, 0, 0)),
        pl.BlockSpec(memory_space=pltpu.SMEM),
        _resident((None, n_seq, HIST_ROWS, POOL_W), const(layer, 0, 0, 0)),
        pl.BlockSpec((None, SEQS_PER_STEP, RET_HEADS, RET_DK, RET_DV),
                     lambda s: (layer, s, 0, 0, 0)),
        _resident((None, 1, D_MODEL), const(layer, 0, 0)),
        _resident((D_MODEL, IN_W), const(0, 0)),
        _resident((None, len(POOL_WINDOWS), POOL_GW, POOL_GW), const(layer, 0, 0, 0)),
        _resident((None, 1, POOL_W), const(layer, 0, 0)),
        _resident((None, 1, RET_W), const(layer, 0, 0)),
        pl.BlockSpec(memory_space=pl.ANY),
    ]
    out_specs = [
        pl.BlockSpec((rows, MIX_W), const(0, 0)),
        pl.BlockSpec((n_seq, POOL_HIST, POOL_W), const(0, 0, 0)),
        pl.BlockSpec((None, SEQS_PER_STEP, RET_HEADS, RET_DK, RET_DV),
                     lambda s: (layer, s, 0, 0, 0)),
    ]
    out_shape = [
        jax.ShapeDtypeStruct((rows, MIX_W), BF16),
        jax.ShapeDtypeStruct((n_seq, POOL_HIST, POOL_W), F32),
        jax.ShapeDtypeStruct(states.shape, F32),
    ]
    scratch_shapes = [
        pltpu.VMEM((n_seq, HIST_ROWS + seq_rows, POOL_W), F32),
        pltpu.VMEM((rows, RET_W), BF16),
        pltpu.VMEM((RET_HEADS, RET_DK, rows), BF16),
        pltpu.VMEM((rows, RET_W), BF16),
        pltpu.VMEM((rows, RET_W), F32),
        pltpu.VMEM((rows, RET_W), F32),
        pltpu.VMEM((rows, RET_W), F32),
    ]
    return pl.pallas_call(
        functools.partial(_mixer_sample_kernel, offset=offset, seq_rows=seq_rows),
        grid=(n_seq // SEQS_PER_STEP,),
        in_specs=in_specs,
        out_specs=out_specs,
        out_shape=out_shape,
        scratch_shapes=scratch_shapes,
        compiler_params=pltpu.CompilerParams(
            dimension_semantics=("arbitrary",),
            vmem_limit_bytes=VMEM_LIMIT_BYTES),
        input_output_aliases={len(in_specs) - 1: 2},
        name=f"mixer_sample_l{layer}",
    )(h2d, cos, sin, dmat, qdec, kdec, cdec, hist0, s0, w["g_mix"], win_bf16, w["w_pool"],
      w["pool_scale"], w["g_gn"], states)


def _output_call(h_p, mix_p, p2d_p, h_s, mix_s, p2d_s, layer, w, wout_bf16, wpg_bf16):
    rows_p, rows_s = h_p.shape[0], h_s.shape[0]
    T = min(OUT_ROW_TILE, rows_p)
    n = rows_p // T
    final_norm = layer == DEPTH - 1
    next_layer = None if final_norm else layer + 1
    const = lambda *idx: (lambda i: idx)
    tile = lambda i: jnp.minimum(i, n - 1)
    in_specs = [
        pl.BlockSpec((T, D_MODEL), lambda i: (tile(i), 0)),
        pl.BlockSpec((T, MIX_W), lambda i: (tile(i), 0)),
        pl.BlockSpec((None, T, PLE_DIM), lambda i: (layer, tile(i), 0)),
        _resident((rows_s, D_MODEL), const(0, 0)),
        _resident((rows_s, MIX_W), const(0, 0)),
        _resident((None, rows_s, PLE_DIM), const(layer, 0, 0)),
        _resident((MIX_W, D_MODEL), const(0, 0)),
        _resident((None, 1, D_MODEL), const(layer, 0, 0)),
        _resident((D_MODEL, D_MODEL), const(0, 0)),
        _resident((None, PLE_DIM, D_MODEL), const(layer, 0, 0)),
        _resident((1, D_MODEL), const(0, 0)),
    ]
    out_specs = [pl.BlockSpec((T, D_MODEL), lambda i: (tile(i), 0)),
                 pl.BlockSpec((rows_s, D_MODEL), const(0, 0))]
    out_shape = [jax.ShapeDtypeStruct((rows_p, D_MODEL), F32),
                 jax.ShapeDtypeStruct((rows_s, D_MODEL), F32)]
    operands = [h_p, mix_p, p2d_p, h_s, mix_s, p2d_s,
                wout_bf16, w["g_ple"], wpg_bf16, w["w_ple"], w["g_final"]]
    scratch_shapes = []
    if next_layer is not None:
        in_specs.append(pl.BlockSpec(memory_space=pl.ANY))
        operands.append(w["w_in"])
        out_specs.append(pl.BlockSpec(memory_space=pl.ANY))
        out_shape.append(jax.ShapeDtypeStruct((D_MODEL, IN_W), BF16))
        scratch_shapes = [
            pltpu.VMEM((D_MODEL // n, IN_W), F32),
            pltpu.VMEM((D_MODEL // n, IN_W), BF16),
            pltpu.SemaphoreType.DMA((2,)),
        ]
    return pl.pallas_call(
        functools.partial(_output_kernel, final_norm=final_norm, next_layer=next_layer,
                          n_prompt_steps=n),
        grid=(n + 1,),
        in_specs=in_specs,
        out_specs=out_specs,
        out_shape=out_shape,
        scratch_shapes=scratch_shapes,
        compiler_params=pltpu.CompilerParams(
            dimension_semantics=("arbitrary",),
            vmem_limit_bytes=VMEM_LIMIT_BYTES),
        name=f"output_l{layer}",
    )(*operands)


def _trunks(x_p, p_p, x_s, p_s, hist0_s, s0_s, offset_s, w):
    B, L, _ = x_p.shape
    Bs, Ls, _ = x_s.shape
    rows_p, rows_s = B * L, Bs * Ls
    tables_p = _retention_tables(RET_CHUNK, RET_CHUNK)
    tables_s = _retention_tables(rows_s, Ls)
    rope_p = _rope_tables(np.arange(L))
    rope_s = _rope_tables(offset_s + np.arange(rows_s) % Ls)
    p2d_p = p_p.reshape(DEPTH, rows_p, PLE_DIM)
    p2d_s = p_s.reshape(DEPTH, rows_s, PLE_DIM)
    hist0_s = jnp.pad(hist0_s, ((0, 0), (0, 0), (HIST_ROWS - POOL_HIST, 0), (0, 0)))
    h_p, h_s = x_p, x_s.reshape(rows_s, D_MODEL)
    hists_p, hists_s = [], []
    states_p = jnp.zeros((DEPTH, B, RET_HEADS, RET_DK, RET_DV), F32)
    states_s = jnp.zeros((DEPTH, Bs, RET_HEADS, RET_DK, RET_DV), F32)
    win_b = w["w_in"][0].astype(BF16)
    for layer in range(DEPTH):
        mix_p, hist, states_p, wout_b, wpg_b = _mixer_prompt(h_p, layer, 0, tables_p, rope_p, w,
                                                             win_b, states_p)
        hists_p.append(hist)
        mix_s, hist, states_s = _mixer_sample(h_s, hist0_s, s0_s, layer, offset_s, Ls, tables_s,
                                              rope_s, w, win_b, states_s)
        hists_s.append(hist)
        outs = _output_call(h_p.reshape(rows_p, D_MODEL), mix_p.reshape(rows_p, MIX_W), p2d_p,
                            h_s, mix_s, p2d_s, layer, w, wout_b, wpg_b)
        h_p, h_s = outs[0].reshape(B, L, D_MODEL), outs[1]
        if layer + 1 < DEPTH:
            win_b = outs[2]
    stack = lambda xs: jnp.stack(xs, axis=0)
    return (h_p, h_s.reshape(Bs, Ls, D_MODEL), stack(hists_p), states_p, stack(hists_s), states_s)


def kernel(x_prompt, x_sample, p_prompt, p_sample, state_pool, state_ret, g_mix, w_in, w_pool,
           pool_scale, g_gn, w_out, g_ple, w_pg, w_ple, g_final):
    past_len = 1024
    row = lambda a: a.reshape(a.shape[0], 1, a.shape[1])
    w = {
        "g_mix": row(g_mix), "w_in": w_in, "w_pool": w_pool.astype(BF16),
        "pool_scale": row(pool_scale), "g_gn": row(g_gn), "w_out": w_out,
        "g_ple": row(g_ple), "w_pg": w_pg, "w_ple": w_ple.astype(BF16),
        "g_final": g_final.reshape(1, D_MODEL),
    }
    return _trunks(x_prompt, p_prompt, x_sample, p_sample, state_pool, state_ret, past_len, w)
```

```python
import functools

import jax
import jax.numpy as jnp
import numpy as np
from jax import lax
from jax.experimental import pallas as pl
from jax.experimental.pallas import tpu as pltpu

D_MODEL = 2048
DEPTH = 4
PLE_DIM = 256
POOL_W = 1024
POOL_WINDOWS = (2, 4, 8, 16)
POOL_GW = POOL_W // len(POOL_WINDOWS)
POOL_HIST = max(POOL_WINDOWS) - 1
HIST_ROWS = POOL_HIST + 1
RET_HEADS = 4
RET_DK = 256
RET_DV = 256
RET_W = RET_HEADS * RET_DV
MIX_W = POOL_W + RET_W
IN_W = 2 * POOL_W + 2 * RET_HEADS * RET_DK + 2 * RET_W
ROPE_BASE = 10000.0
EPS = 1e-6
GN_EPS = 1e-5
HALF = RET_DK // 2

RET_CHUNK = 256
ROW_TILE = 512
OUT_ROW_TILE = 512
SEQS_PER_STEP = 2
VMEM_LIMIT_BYTES = 56 * 1024 * 1024

F32 = jnp.float32
BF16 = jnp.bfloat16


def _resident(block_shape, index_map):
    return pl.BlockSpec(block_shape, index_map, pipeline_mode=pl.Buffered(1))


def _rmsnorm(x, g):
    ms = jnp.mean(x * x, axis=-1, keepdims=True)
    return x * lax.rsqrt(ms + EPS) * g


def _silu(x):
    return x * jax.nn.sigmoid(x)


def _dot(a, b):
    return jnp.dot(a, b, preferred_element_type=F32)


def _dot_nt(a, b):
    return lax.dot_general(a, b, (((1,), (1,)), ((), ())), preferred_element_type=F32)


def _dot_tn(a, b):
    return lax.dot_general(a, b, (((0,), (0,)), ((), ())), preferred_element_type=F32)


def _rotate(x, cos, sin):
    x1, x2 = x[:, :HALF], x[:, HALF:]
    return jnp.concatenate([x1 * cos - x2 * sin, x1 * sin + x2 * cos], axis=-1)


def _both_halves(ref, hd):
    return jnp.concatenate([ref[hd], ref[hd]], axis=-1)


def _pool_inputs(ubuf_ref, pos, seq_rows):
    n_seq = ubuf_ref.shape[0]
    rows = n_seq * seq_rows
    pooled = []
    for g, w in enumerate(POOL_WINDOWS):
        cols = slice(g * POOL_GW, (g + 1) * POOL_GW)
        u_g = ubuf_ref[:, HIST_ROWS:HIST_ROWS + seq_rows, cols]
        acc = u_g
        for j in range(1, w):
            acc = acc + ubuf_ref[:, HIST_ROWS - j:HIST_ROWS - j + seq_rows, cols]
        acc = acc.reshape(rows, POOL_GW)
        u_g = u_g.reshape(rows, POOL_GW)
        cnt = jnp.minimum(pos + 1, w).astype(F32)
        pooled.append((acc / cnt - u_g).astype(BF16))
    return pooled


def _pool_outputs(pooled, gp, wpool_ref, pscale_ref, mix_ref):
    for g in range(len(POOL_WINDOWS)):
        cols = slice(g * POOL_GW, (g + 1) * POOL_GW)
        y = _dot(pooled[g], wpool_ref[g])
        y = y * pscale_ref[:, cols] * _silu(gp[:, cols])
        mix_ref[:, cols] = y.astype(mix_ref.dtype)


def _group_norm_gate(o, ggn, gate):
    mu = jnp.mean(o, axis=-1, keepdims=True)
    d = o - mu
    var = jnp.mean(d * d, axis=-1, keepdims=True)
    return d * lax.rsqrt(var + GN_EPS) * ggn * gate


class _SideConvert:
    def __init__(self, src_hbm, layer, dst_hbm, in_ref, out_ref, sems, step, n_steps):
        self.src, self.layer, self.dst = src_hbm, layer, dst_hbm
        self.in_ref, self.out_ref, self.sems = in_ref, out_ref, sems
        self.step, self.n_steps = step, n_steps
        self.rows = in_ref.shape[0]

    def _load(self, j):
        r0 = pl.multiple_of(j * self.rows, self.rows)
        return pltpu.make_async_copy(self.src.at[self.layer, pl.ds(r0, self.rows), :],
                                     self.in_ref, self.sems.at[0])

    def _store(self, j):
        r0 = pl.multiple_of(j * self.rows, self.rows)
        return pltpu.make_async_copy(self.out_ref, self.dst.at[pl.ds(r0, self.rows), :],
                                     self.sems.at[1])

    def head(self):
        @pl.when(self.step == 0)
        def _():
            self._load(0).start()

        @pl.when((self.step > 0) & (self.step < self.n_steps))
        def _():
            self._store(self.step - 1).wait()

        @pl.when(self.step < self.n_steps)
        def _():
            self._load(self.step).wait()
            self.out_ref[...] = self.in_ref[...].astype(self.out_ref.dtype)
            self._store(self.step).start()
            self._load(jnp.minimum(self.step + 1, self.n_steps - 1)).start()

    def tail(self):
        @pl.when(self.step == self.n_steps - 1)
        def _():
            self._store(self.step).wait()
            self._load(self.step).wait()


class _ZeroFill:
    def __init__(self, dst_hbm, zero_ref, sem, step, n_steps):
        self.dst, self.zero_ref, self.sem = dst_hbm, zero_ref, sem
        self.step, self.n_steps = step, n_steps
        self.per_layer = dst_hbm.shape[1] // zero_ref.shape[0]

    def _copy(self, j):
        k = self.zero_ref.shape[0]
        layer = j // self.per_layer
        i0 = pl.multiple_of((j - layer * self.per_layer) * k, k)
        return pltpu.make_async_copy(self.zero_ref, self.dst.at[layer, pl.ds(i0, k)],
                                     self.sem.at[0])

    def head(self):
        @pl.when(self.step == 0)
        def _():
            self.zero_ref[...] = jnp.zeros(self.zero_ref.shape, self.zero_ref.dtype)

        @pl.when(self.step > 0)
        def _():
            self._copy(self.step - 1).wait()

        @pl.when(self.step < self.n_steps)
        def _():
            self._copy(self.step).start()

    def tail(self):
        @pl.when(self.step == self.n_steps - 1)
        def _():
            self._copy(self.step).wait()


def _mixer_prompt_kernel(h_ref, cos_ref, sin_ref, dmat_ref, qdec_ref, kdec_ref, cdec_ref,
                         gmix_ref, win_ref, wpool_ref, pscale_ref, ggn_ref, wout_hbm, wpg_hbm,
                         states_hbm,
                         mix_ref, hist_ref, s_ref, wout_bf16_hbm, wpg_bf16_hbm, *rest,
                         offset, layer, fill):
    del states_hbm
    T = h_ref.shape[0]
    C = dmat_ref.shape[1]
    t = pl.program_id(1)
    step = pl.program_id(0) * pl.num_programs(1) + t
    n_steps = pl.num_programs(0) * pl.num_programs(1)
    if fill:
        (fill_hbm, ubuf_ref, wout_in, wout_out, wout_sems, wpg_in, wpg_out, wpg_sems,
         zero_ref, fill_sem) = rest
    else:
        ubuf_ref, wout_in, wout_out, wout_sems, wpg_in, wpg_out, wpg_sems = rest
    sides = [
        _SideConvert(wout_hbm, layer, wout_bf16_hbm, wout_in, wout_out, wout_sems, step, n_steps),
        _SideConvert(wpg_hbm, layer, wpg_bf16_hbm, wpg_in, wpg_out, wpg_sems, step, n_steps),
    ]
    if fill:
        sides.append(_ZeroFill(fill_hbm, zero_ref, fill_sem, step, n_steps))
    for side in sides:
        side.head()

    @pl.when(t == 0)
    def _():
        ubuf_ref[:, :HIST_ROWS, :] = jnp.zeros((1, HIST_ROWS, POOL_W), F32)
        s_ref[...] = jnp.zeros(s_ref.shape, F32)

    for c in range(T // C):
        rows = slice(c * C, (c + 1) * C)
        hn = _rmsnorm(h_ref[rows, :], gmix_ref[...]).astype(BF16)

        def proj(j, hn=hn):
            return _dot(hn, win_ref[:, j * POOL_W:(j + 1) * POOL_W])

        ubuf_ref[0, HIST_ROWS + c * C:HIST_ROWS + (c + 1) * C, :] = proj(0)
        gp = proj(1)
        pos = offset + t * T + c * C + lax.broadcasted_iota(jnp.int32, (C, POOL_GW), 0)
        pooled = _pool_inputs(ubuf_ref.at[:, c * C:HIST_ROWS + (c + 1) * C, :], pos, C)
        q, k, v, gr = proj(2), proj(3), proj(4).astype(BF16), proj(5)
        _pool_outputs(pooled, gp, wpool_ref, pscale_ref, mix_ref.at[rows, :])

        cos, sin = cos_ref[rows, :], sin_ref[rows, :]
        head_cols = [slice(hd * RET_DK, (hd + 1) * RET_DK) for hd in range(RET_HEADS)]
        scores, cross = [], []
        for hd, cols in enumerate(head_cols):
            qr = _rotate(q[:, cols], cos, sin).astype(BF16)
            kr = _rotate(k[:, cols], cos, sin) * (RET_DK ** -0.5)
            kd = (kr * _both_halves(kdec_ref, hd)).astype(BF16)
            scores.append(_dot_nt(qr, kr.astype(BF16)))
            s_old = s_ref[hd]
            cross.append(_dot(qr, s_old.astype(BF16)) * _both_halves(qdec_ref, hd))
            s_ref[hd] = s_old * cdec_ref[hd] + _dot_tn(kd, v[:, cols])
        for hd, cols in enumerate(head_cols):
            p = (scores[hd] * dmat_ref[hd]).astype(BF16)
            o = _dot(p, v[:, cols]) + cross[hd]
            on = _group_norm_gate(o, ggn_ref[:, cols], _silu(gr[:, cols]))
            mix_ref[rows, POOL_W + hd * RET_DV:POOL_W + (hd + 1) * RET_DV] = on.astype(mix_ref.dtype)

    @pl.when(t == pl.num_programs(1) - 1)
    def _():
        hist_ref[...] = ubuf_ref[0, T + 1:T + HIST_ROWS, :]

    ubuf_ref[0, :HIST_ROWS, :] = ubuf_ref[0, T:T + HIST_ROWS, :]

    for side in sides:
        side.tail()


def _mixer_sample_kernel(h_ref, cos_ref, sin_ref, dmat_ref, qdec_ref, kdec_ref, cdec_ref,
                         hist0_ref, s0_ref, gmix_ref, win_ref, wpool_ref, pscale_ref, ggn_ref,
                         states_hbm,
                         mix_ref, hist_ref, s_ref,
                         ubuf_ref, qr_ref, kdt_ref, v_ref, oin_ref, ox_ref, gate_ref,
                         *, offset, seq_rows):
    del states_hbm
    rows = h_ref.shape[0]
    s = pl.program_id(0)

    @pl.when(s == 0)
    def _():
        n_seq = rows // seq_rows
        hn = _rmsnorm(h_ref[...], gmix_ref[...]).astype(BF16)

        def proj(j):
            return _dot(hn, win_ref[:, j * POOL_W:(j + 1) * POOL_W])

        ubuf_ref[:, :HIST_ROWS, :] = hist0_ref[...]
        ubuf_ref[:, HIST_ROWS:, :] = proj(0).reshape(n_seq, seq_rows, POOL_W)
        gp = proj(1)
        row = lax.broadcasted_iota(jnp.int32, (rows, POOL_GW), 0)
        pos = offset + lax.rem(row, seq_rows)
        pooled = _pool_inputs(ubuf_ref, pos, seq_rows)
        hist_ref[...] = ubuf_ref[:, seq_rows + 1:seq_rows + HIST_ROWS, :]
        q, k = proj(2), proj(3)
        _pool_outputs(pooled, gp, wpool_ref, pscale_ref, mix_ref)

        v_ref[...] = proj(4).astype(BF16)
        gate_ref[...] = _silu(proj(5))
        cos, sin = cos_ref[...], sin_ref[...]
        for hd in range(RET_HEADS):
            cols = slice(hd * RET_DK, (hd + 1) * RET_DK)
            qr = _rotate(q[:, cols], cos, sin).astype(BF16)
            kr = _rotate(k[:, cols], cos, sin) * (RET_DK ** -0.5)
            qr_ref[:, cols] = qr
            kdt_ref[hd] = (kr * _both_halves(kdec_ref, hd)).T.astype(BF16)
            p = (_dot_nt(qr, kr.astype(BF16)) * dmat_ref[hd]).astype(BF16)
            oin_ref[:, cols] = _dot(p, v_ref[:, cols])

    row_id = lax.broadcasted_iota(jnp.int32, (rows, RET_DV), 0)
    for i in range(s0_ref.shape[0]):
        r0 = pl.multiple_of((s * s0_ref.shape[0] + i) * seq_rows, seq_rows)
        in_seq = (row_id >= r0) & (row_id < r0 + seq_rows)
        for hd in range(RET_HEADS):
            cols = slice(hd * RET_DK, (hd + 1) * RET_DK)
            s_old = s0_ref[i, hd]
            ox_ref[pl.ds(r0, seq_rows), cols] = _dot(qr_ref[pl.ds(r0, seq_rows), cols],
                                                     s_old.astype(BF16))
            vh = v_ref[:, cols]
            v_seq = jnp.where(in_seq, vh, jnp.zeros_like(vh))
            s_ref[i, hd] = s_old * cdec_ref[hd] + _dot(kdt_ref[hd], v_seq)

    @pl.when(s == pl.num_programs(0) - 1)
    def _():
        for hd in range(RET_HEADS):
            cols = slice(hd * RET_DK, (hd + 1) * RET_DK)
            o = oin_ref[:, cols] + ox_ref[:, cols] * _both_halves(qdec_ref, hd)
            on = _group_norm_gate(o, ggn_ref[:, cols], gate_ref[:, cols])
            mix_ref[:, POOL_W + hd * RET_DV:POOL_W + (hd + 1) * RET_DV] = on.astype(mix_ref.dtype)


def _output_body(h_ref, mix_ref, p_ref, wout_ref, gple_ref, wpg_ref, wple_ref, gfin_ref,
                 out_ref, final_norm):
    h = h_ref[...] + _dot(mix_ref[...], wout_ref[...])
    hn = _rmsnorm(h, gple_ref[...]).astype(BF16)
    gate = jax.nn.sigmoid(_dot(hn, wpg_ref[...]))
    h = h + gate * _dot(p_ref[...].astype(BF16), wple_ref[...])
    if final_norm:
        h = _rmsnorm(h, gfin_ref[...])
    out_ref[...] = h


def _output_kernel(h_ref, mix_ref, p_ref, hs_ref, mixs_ref, ps_ref,
                   wout_ref, gple_ref, wpg_ref, wple_ref, gfin_ref, *rest,
                   final_norm, next_layer, n_prompt_steps):
    i = pl.program_id(0)
    side = None
    if next_layer is None:
        out_ref, outs_ref = rest
    else:
        win_next_hbm, out_ref, outs_ref, win_next_bf16_hbm, win_in, win_out, side_sems = rest
        side = _SideConvert(win_next_hbm, next_layer, win_next_bf16_hbm, win_in, win_out,
                            side_sems, i, n_prompt_steps)
        side.head()

    @pl.when(i == n_prompt_steps)
    def _():
        _output_body(hs_ref, mixs_ref, ps_ref, wout_ref, gple_ref, wpg_ref, wple_ref, gfin_ref,
                     outs_ref, final_norm)

    @pl.when(i != n_prompt_steps)
    def _():
        _output_body(h_ref, mix_ref, p_ref, wout_ref, gple_ref, wpg_ref, wple_ref, gfin_ref,
                     out_ref, final_norm)

    if side is not None:
        side.tail()


def _retention_tables(rows, seq_rows):
    lg = np.log(1.0 - 2.0 ** (-5.0 - np.arange(RET_HEADS, dtype=np.float64)))
    n = np.arange(rows)
    r = (n % seq_rows).astype(np.float64)
    same = (n // seq_rows)[:, None] == (n // seq_rows)[None, :]
    diff = r[:, None] - r[None, :]
    keep = same & (diff >= 0)
    dmat = np.where(keep[None], np.exp(np.maximum(diff, 0.0)[None] * lg[:, None, None]), 0.0)
    qdec = np.exp((r[None, :] + 1.0) * lg[:, None])
    kdec = np.exp((seq_rows - 1.0 - r[None, :]) * lg[:, None])
    cdec = np.exp(seq_rows * lg)
    bcast = lambda a: np.broadcast_to(a[:, :, None], (RET_HEADS, rows, HALF))
    return tuple(jnp.asarray(a, F32) for a in (dmat, bcast(qdec), bcast(kdec), cdec))


def _rope_tables(pos):
    inv = 1.0 / (ROPE_BASE ** (np.arange(HALF, dtype=np.float64) / HALF))
    ang = pos.astype(np.float64)[:, None] * inv[None, :]
    return jnp.asarray(np.cos(ang), F32), jnp.asarray(np.sin(ang), F32)


def _mixer_prompt(h, layer, offset, tables, rope, w, win_bf16, states, fill_shape=None):
    B, L, _ = h.shape
    T = ROW_TILE
    n_steps = B * (L // T)
    dmat, qdec, kdec, cdec = tables
    cos, sin = rope
    const = lambda *idx: (lambda b, t: idx)
    in_specs = [
        pl.BlockSpec((None, T, D_MODEL), lambda b, t: (b, t, 0)),
        pl.BlockSpec((T, HALF), lambda b, t: (t, 0)),
        pl.BlockSpec((T, HALF), lambda b, t: (t, 0)),
        _resident((RET_HEADS, RET_CHUNK, RET_CHUNK), const(0, 0, 0)),
        _resident((RET_HEADS, RET_CHUNK, HALF), const(0, 0, 0)),
        _resident((RET_HEADS, RET_CHUNK, HALF), const(0, 0, 0)),
        pl.BlockSpec(memory_space=pltpu.SMEM),
        _resident((None, 1, D_MODEL), const(layer, 0, 0)),
        _resident((D_MODEL, IN_W), const(0, 0)),
        _resident((None, len(POOL_WINDOWS), POOL_GW, POOL_GW), const(layer, 0, 0, 0)),
        _resident((None, 1, POOL_W), const(layer, 0, 0)),
        _resident((None, 1, RET_W), const(layer, 0, 0)),
        pl.BlockSpec(memory_space=pl.ANY),
        pl.BlockSpec(memory_space=pl.ANY),
        pl.BlockSpec(memory_space=pl.ANY),
    ]
    out_specs = [
        pl.BlockSpec((None, T, MIX_W), lambda b, t: (b, t, 0)),
        pl.BlockSpec((None, POOL_HIST, POOL_W), lambda b, t: (b, 0, 0)),
        pl.BlockSpec((None, None, RET_HEADS, RET_DK, RET_DV), lambda b, t: (layer, b, 0, 0, 0)),
        pl.BlockSpec(memory_space=pl.ANY),
        pl.BlockSpec(memory_space=pl.ANY),
    ]
    out_shape = [
        jax.ShapeDtypeStruct((B, L, MIX_W), BF16),
        jax.ShapeDtypeStruct((B, POOL_HIST, POOL_W), F32),
        jax.ShapeDtypeStruct(states.shape, F32),
        jax.ShapeDtypeStruct((MIX_W, D_MODEL), BF16),
        jax.ShapeDtypeStruct((D_MODEL, D_MODEL), BF16),
    ]
    scratch_shapes = [
        pltpu.VMEM((1, HIST_ROWS + T, POOL_W), F32),
        pltpu.VMEM((MIX_W // n_steps, D_MODEL), F32),
        pltpu.VMEM((MIX_W // n_steps, D_MODEL), BF16),
        pltpu.SemaphoreType.DMA((2,)),
        pltpu.VMEM((D_MODEL // n_steps, D_MODEL), F32),
        pltpu.VMEM((D_MODEL // n_steps, D_MODEL), BF16),
        pltpu.SemaphoreType.DMA((2,)),
    ]
    if fill_shape is not None:
        out_specs.append(pl.BlockSpec(memory_space=pl.ANY))
        out_shape.append(jax.ShapeDtypeStruct(fill_shape, F32))
        slab = fill_shape[0] * fill_shape[1] // n_steps
        scratch_shapes += [pltpu.VMEM((slab,) + tuple(fill_shape[2:]), F32),
                           pltpu.SemaphoreType.DMA((1,))]
    return pl.pallas_call(
        functools.partial(_mixer_prompt_kernel, offset=offset, layer=layer,
                          fill=fill_shape is not None),
        grid=(B, L // T),
        in_specs=in_specs,
        out_specs=out_specs,
        out_shape=out_shape,
        scratch_shapes=scratch_shapes,
        compiler_params=pltpu.CompilerParams(
            dimension_semantics=("arbitrary", "arbitrary"),
            vmem_limit_bytes=VMEM_LIMIT_BYTES),
        input_output_aliases={len(in_specs) - 1: 2},
        name=f"mixer_prompt_l{layer}",
    )(h, cos, sin, dmat, qdec, kdec, cdec, w["g_mix"], win_bf16, w["w_pool"],
      w["pool_scale"], w["g_gn"], w["w_out"], w["w_pg"], states)


def _mixer_sample(h2d, hist0, s0, layer, offset, seq_rows, tables, rope, w, win_bf16, states):
    rows = h2d.shape[0]
    n_seq = rows // seq_rows
    dmat, qdec, kdec, cdec = tables
    cos, sin = rope
    const = lambda *idx: (lambda s: idx)
    in_specs = [
        _resident((rows, D_MODEL), const(0, 0)),
        _resident((rows, HALF), const(0, 0)),
        _resident((rows, HALF), const(0, 0)),
        _resident((RET_HEADS, rows, rows), const(0, 0, 0)),
        _resident((RET_HEADS, rows, HALF), const(0, 0, 0)),
        _resident((RET_HEADS, rows, HALF), const(0, 0, 0)),
        pl.BlockSpec(memory_space=pltpu.SMEM),
        _resident((None, n_seq, HIST_ROWS, POOL_W), const(layer, 0, 0, 0)),
        pl.BlockSpec((None, SEQS_PER_STEP, RET_HEADS, RET_DK, RET_DV),
                     lambda s: (layer, s, 0, 0, 0)),
        _resident((None, 1, D_MODEL), const(layer, 0, 0)),
        _resident((D_MODEL, IN_W), const(0, 0)),
        _resident((None, len(POOL_WINDOWS), POOL_GW, POOL_GW), const(layer, 0, 0, 0)),
        _resident((None, 1, POOL_W), const(layer, 0, 0)),
        _resident((None, 1, RET_W), const(layer, 0, 0)),
        pl.BlockSpec(memory_space=pl.ANY),
    ]
    out_specs = [
        pl.BlockSpec((rows, MIX_W), const(0, 0)),
        pl.BlockSpec((n_seq, POOL_HIST, POOL_W), const(0, 0, 0)),
        pl.BlockSpec((None, SEQS_PER_STEP, RET_HEADS, RET_DK, RET_DV),
                     lambda s: (layer, s, 0, 0, 0)),
    ]
    out_shape = [
        jax.ShapeDtypeStruct((rows, MIX_W), BF16),
        jax.ShapeDtypeStruct((n_seq, POOL_HIST, POOL_W), F32),
        jax.ShapeDtypeStruct(states.shape, F32),
    ]
    scratch_shapes = [
        pltpu.VMEM((n_seq, HIST_ROWS + seq_rows, POOL_W), F32),
        pltpu.VMEM((rows, RET_W), BF16),
        pltpu.VMEM((RET_HEADS, RET_DK, rows), BF16),
        pltpu.VMEM((rows, RET_W), BF16),
        pltpu.VMEM((rows, RET_W), F32),
        pltpu.VMEM((rows, RET_W), F32),
        pltpu.VMEM((rows, RET_W), F32),
    ]
    return pl.pallas_call(
        functools.partial(_mixer_sample_kernel, offset=offset, seq_rows=seq_rows),
        grid=(n_seq // SEQS_PER_STEP,),
        in_specs=in_specs,
        out_specs=out_specs,
        out_shape=out_shape,
        scratch_shapes=scratch_shapes,
        compiler_params=pltpu.CompilerParams(
            dimension_semantics=("arbitrary",),
            vmem_limit_bytes=VMEM_LIMIT_BYTES),
        input_output_aliases={len(in_specs) - 1: 2},
        name=f"mixer_sample_l{layer}",
    )(h2d, cos, sin, dmat, qdec, kdec, cdec, hist0, s0, w["g_mix"], win_bf16, w["w_pool"],
      w["pool_scale"], w["g_gn"], states)


def _output_call(h_p, mix_p, p2d_p, h_s, mix_s, p2d_s, layer, w, wout_bf16, wpg_bf16):
    rows_p, rows_s = h_p.shape[0], h_s.shape[0]
    T = min(OUT_ROW_TILE, rows_p)
    n = rows_p // T
    final_norm = layer == DEPTH - 1
    next_layer = None if final_norm else layer + 1
    const = lambda *idx: (lambda i: idx)
    tile = lambda i: jnp.minimum(i, n - 1)
    in_specs = [
        pl.BlockSpec((T, D_MODEL), lambda i: (tile(i), 0)),
        pl.BlockSpec((T, MIX_W), lambda i: (tile(i), 0)),
        pl.BlockSpec((None, T, PLE_DIM), lambda i: (layer, tile(i), 0)),
        _resident((rows_s, D_MODEL), const(0, 0)),
        _resident((rows_s, MIX_W), const(0, 0)),
        _resident((None, rows_s, PLE_DIM), const(layer, 0, 0)),
        _resident((MIX_W, D_MODEL), const(0, 0)),
        _resident((None, 1, D_MODEL), const(layer, 0, 0)),
        _resident((D_MODEL, D_MODEL), const(0, 0)),
        _resident((None, PLE_DIM, D_MODEL), const(layer, 0, 0)),
        _resident((1, D_MODEL), const(0, 0)),
    ]
    out_specs = [pl.BlockSpec((T, D_MODEL), lambda i: (tile(i), 0)),
                 pl.BlockSpec((rows_s, D_MODEL), const(0, 0))]
    out_shape = [jax.ShapeDtypeStruct((rows_p, D_MODEL), F32),
                 jax.ShapeDtypeStruct((rows_s, D_MODEL), F32)]
    operands = [h_p, mix_p, p2d_p, h_s, mix_s, p2d_s,
                wout_bf16, w["g_ple"], wpg_bf16, w["w_ple"], w["g_final"]]
    scratch_shapes = []
    if next_layer is not None:
        in_specs.append(pl.BlockSpec(memory_space=pl.ANY))
        operands.append(w["w_in"])
        out_specs.append(pl.BlockSpec(memory_space=pl.ANY))
        out_shape.append(jax.ShapeDtypeStruct((D_MODEL, IN_W), BF16))
        scratch_shapes = [
            pltpu.VMEM((D_MODEL // n, IN_W), F32),
            pltpu.VMEM((D_MODEL // n, IN_W), BF16),
            pltpu.SemaphoreType.DMA((2,)),
        ]
    return pl.pallas_call(
        functools.partial(_output_kernel, final_norm=final_norm, next_layer=next_layer,
                          n_prompt_steps=n),
        grid=(n + 1,),
        in_specs=in_specs,
        out_specs=out_specs,
        out_shape=out_shape,
        scratch_shapes=scratch_shapes,
        compiler_params=pltpu.CompilerParams(
            dimension_semantics=("arbitrary",),
            vmem_limit_bytes=VMEM_LIMIT_BYTES),
        name=f"output_l{layer}",
    )(*operands)


def _trunks(x_p, p_p, x_s, p_s, hist0_s, s0_s, offset_s, w):
    B, L, _ = x_p.shape
    Bs, Ls, _ = x_s.shape
    rows_p, rows_s = B * L, Bs * Ls
    tables_p = _retention_tables(RET_CHUNK, RET_CHUNK)
    tables_s = _retention_tables(rows_s, Ls)
    rope_p = _rope_tables(np.arange(L))
    rope_s = _rope_tables(offset_s + np.arange(rows_s) % Ls)
    p2d_p = p_p.reshape(DEPTH, rows_p, PLE_DIM)
    p2d_s = p_s.reshape(DEPTH, rows_s, PLE_DIM)
    hist0_s = jnp.pad(hist0_s, ((0, 0), (0, 0), (HIST_ROWS - POOL_HIST, 0), (0, 0)))
    h_p, h_s = x_p, x_s.reshape(rows_s, D_MODEL)
    hists_p, hists_s = [], []
    states_p = jnp.zeros((DEPTH, B, RET_HEADS, RET_DK, RET_DV), F32)
    states_s_shape = (DEPTH, Bs, RET_HEADS, RET_DK, RET_DV)
    n_steps = B * (L // ROW_TILE)
    slab = DEPTH * Bs // n_steps
    fill = slab > 0 and slab * n_steps == DEPTH * Bs and Bs % slab == 0
    states_s = None if fill else jnp.zeros(states_s_shape, F32)
    win_b = w["w_in"][0].astype(BF16)
    for layer in range(DEPTH):
        outs = _mixer_prompt(h_p, layer, 0, tables_p, rope_p, w, win_b, states_p,
                             fill_shape=states_s_shape if fill and layer == 0 else None)
        mix_p, hist, states_p, wout_b, wpg_b = outs[:5]
        if fill and layer == 0:
            states_s = outs[5]
        hists_p.append(hist)
        mix_s, hist, states_s = _mixer_sample(h_s, hist0_s, s0_s, layer, offset_s, Ls, tables_s,
                                              rope_s, w, win_b, states_s)
        hists_s.append(hist)
        outs = _output_call(h_p.reshape(rows_p, D_MODEL), mix_p.reshape(rows_p, MIX_W), p2d_p,
                            h_s, mix_s, p2d_s, layer, w, wout_b, wpg_b)
        h_p, h_s = outs[0].reshape(B, L, D_MODEL), outs[1]
        if layer + 1 < DEPTH:
            win_b = outs[2]
    stack = lambda xs: jnp.stack(xs, axis=0)
    return (h_p, h_s.reshape(Bs, Ls, D_MODEL), stack(hists_p), states_p, stack(hists_s), states_s)


def kernel(x_prompt, x_sample, p_prompt, p_sample, state_pool, state_ret, g_mix, w_in, w_pool,
           pool_scale, g_gn, w_out, g_ple, w_pg, w_ple, g_final):
    past_len = 1024
    row = lambda a: a.reshape(a.shape[0], 1, a.shape[1])
    w = {
        "g_mix": row(g_mix), "w_in": w_in, "w_pool": w_pool.astype(BF16),
        "pool_scale": row(pool_scale), "g_gn": row(g_gn), "w_out": w_out,
        "g_ple": row(g_ple), "w_pg": w_pg, "w_ple": w_ple.astype(BF16),
        "g_final": g_final.reshape(1, D_MODEL),
    }
    return _trunks(x_prompt, p_prompt, x_sample, p_sample, state_pool, state_ret, past_len, w)
```

```python
import functools

import jax
import jax.numpy as jnp
import numpy as np
from jax import lax
from jax.experimental import pallas as pl
from jax.experimental.pallas import tpu as pltpu

D_MODEL = 2048
DEPTH = 4
PLE_DIM = 256
POOL_W = 1024
POOL_WINDOWS = (2, 4, 8, 16)
POOL_GW = POOL_W // len(POOL_WINDOWS)
POOL_HIST = max(POOL_WINDOWS) - 1
HIST_ROWS = POOL_HIST + 1
RET_HEADS = 4
RET_DK = 256
RET_DV = 256
RET_W = RET_HEADS * RET_DV
MIX_W = POOL_W + RET_W
IN_W = 2 * POOL_W + 2 * RET_HEADS * RET_DK + 2 * RET_W
ROPE_BASE = 10000.0
EPS = 1e-6
GN_EPS = 1e-5
HALF = RET_DK // 2

RET_CHUNK = 256
ROW_TILE = 512
OUT_ROW_TILE = 512
SEQS_PER_STEP = 2
VMEM_LIMIT_BYTES = 56 * 1024 * 1024

F32 = jnp.float32
BF16 = jnp.bfloat16


def _resident(block_shape, index_map):
    return pl.BlockSpec(block_shape, index_map, pipeline_mode=pl.Buffered(1))


def _rmsnorm(x, g):
    ms = jnp.mean(x * x, axis=-1, keepdims=True)
    return x * lax.rsqrt(ms + EPS) * g


def _silu(x):
    return x * jax.nn.sigmoid(x)


def _dot(a, b):
    return jnp.dot(a, b, preferred_element_type=F32)


def _dot_nt(a, b):
    return lax.dot_general(a, b, (((1,), (1,)), ((), ())), preferred_element_type=F32)


def _dot_tn(a, b):
    return lax.dot_general(a, b, (((0,), (0,)), ((), ())), preferred_element_type=F32)


def _rotate(x, cos, sin):
    x1, x2 = x[:, :HALF], x[:, HALF:]
    return jnp.concatenate([x1 * cos - x2 * sin, x1 * sin + x2 * cos], axis=-1)


def _both_halves(ref, hd):
    return jnp.concatenate([ref[hd], ref[hd]], axis=-1)


def _pool_inputs(ubuf_ref, pos, seq_rows):
    n_seq = ubuf_ref.shape[0]
    rows = n_seq * seq_rows
    pooled = []
    for g, w in enumerate(POOL_WINDOWS):
        cols = slice(g * POOL_GW, (g + 1) * POOL_GW)
        u_g = ubuf_ref[:, HIST_ROWS:HIST_ROWS + seq_rows, cols]
        acc = u_g
        for j in range(1, w):
            acc = acc + ubuf_ref[:, HIST_ROWS - j:HIST_ROWS - j + seq_rows, cols]
        acc = acc.reshape(rows, POOL_GW)
        u_g = u_g.reshape(rows, POOL_GW)
        cnt = jnp.minimum(pos + 1, w).astype(F32)
        pooled.append((acc / cnt - u_g).astype(BF16))
    return pooled


def _pool_outputs(pooled, gp, wpool_ref, pscale_ref, mix_ref):
    for g in range(len(POOL_WINDOWS)):
        cols = slice(g * POOL_GW, (g + 1) * POOL_GW)
        y = _dot(pooled[g], wpool_ref[g].astype(BF16))
        y = y * pscale_ref[:, cols] * _silu(gp[:, cols])
        mix_ref[:, cols] = y.astype(mix_ref.dtype)


def _group_norm_gate(o, ggn, gate):
    mu = jnp.mean(o, axis=-1, keepdims=True)
    d = o - mu
    var = jnp.mean(d * d, axis=-1, keepdims=True)
    return d * lax.rsqrt(var + GN_EPS) * ggn * gate


class _SideConvert:
    def __init__(self, src_hbm, layer, dst_hbm, in_ref, out_ref, sems, step, n_steps):
        self.src, self.layer, self.dst = src_hbm, layer, dst_hbm
        self.in_ref, self.out_ref, self.sems = in_ref, out_ref, sems
        self.step, self.n_steps = step, n_steps
        self.rows = in_ref.shape[0]

    def _load(self, j):
        r0 = pl.multiple_of(j * self.rows, self.rows)
        return pltpu.make_async_copy(self.src.at[self.layer, pl.ds(r0, self.rows), :],
                                     self.in_ref, self.sems.at[0])

    def _store(self, j):
        r0 = pl.multiple_of(j * self.rows, self.rows)
        return pltpu.make_async_copy(self.out_ref, self.dst.at[pl.ds(r0, self.rows), :],
                                     self.sems.at[1])

    def head(self):
        @pl.when(self.step == 0)
        def _():
            self._load(0).start()

        @pl.when((self.step > 0) & (self.step < self.n_steps))
        def _():
            self._store(self.step - 1).wait()

        @pl.when(self.step < self.n_steps)
        def _():
            self._load(self.step).wait()
            self.out_ref[...] = self.in_ref[...].astype(self.out_ref.dtype)
            self._store(self.step).start()
            self._load(jnp.minimum(self.step + 1, self.n_steps - 1)).start()

    def tail(self):
        @pl.when(self.step == self.n_steps - 1)
        def _():
            self._store(self.step).wait()
            self._load(self.step).wait()


class _ZeroFill:
    def __init__(self, dst_hbm, zero_ref, sem, step, n_steps):
        self.dst, self.zero_ref, self.sem = dst_hbm, zero_ref, sem
        self.step, self.n_steps = step, n_steps
        self.per_layer = dst_hbm.shape[1] // zero_ref.shape[0]

    def _copy(self, j):
        k = self.zero_ref.shape[0]
        layer = j // self.per_layer
        i0 = pl.multiple_of((j - layer * self.per_layer) * k, k)
        return pltpu.make_async_copy(self.zero_ref, self.dst.at[layer, pl.ds(i0, k)],
                                     self.sem.at[0])

    def head(self):
        @pl.when(self.step == 0)
        def _():
            self.zero_ref[...] = jnp.zeros(self.zero_ref.shape, self.zero_ref.dtype)

        @pl.when(self.step > 0)
        def _():
            self._copy(self.step - 1).wait()

        @pl.when(self.step < self.n_steps)
        def _():
            self._copy(self.step).start()

    def tail(self):
        @pl.when(self.step == self.n_steps - 1)
        def _():
            self._copy(self.step).wait()


def _mixer_prompt_kernel(h_ref, cos_ref, sin_ref, dmat_ref, qdec_ref, kdec_ref, cdec_ref,
                         gmix_ref, win_ref, wpool_ref, pscale_ref, ggn_ref, wout_hbm, wpg_hbm,
                         states_hbm,
                         mix_ref, hist_ref, s_ref, wout_bf16_hbm, wpg_bf16_hbm, *rest,
                         offset, layer, fill):
    del states_hbm
    T = h_ref.shape[0]
    C = dmat_ref.shape[1]
    t = pl.program_id(1)
    step = pl.program_id(0) * pl.num_programs(1) + t
    n_steps = pl.num_programs(0) * pl.num_programs(1)
    if fill:
        (fill_hbm, ubuf_ref, wout_in, wout_out, wout_sems, wpg_in, wpg_out, wpg_sems,
         zero_ref, fill_sem) = rest
    else:
        ubuf_ref, wout_in, wout_out, wout_sems, wpg_in, wpg_out, wpg_sems = rest
    sides = [
        _SideConvert(wout_hbm, layer, wout_bf16_hbm, wout_in, wout_out, wout_sems, step, n_steps),
        _SideConvert(wpg_hbm, layer, wpg_bf16_hbm, wpg_in, wpg_out, wpg_sems, step, n_steps),
    ]
    if fill:
        sides.append(_ZeroFill(fill_hbm, zero_ref, fill_sem, step, n_steps))
    for side in sides:
        side.head()

    @pl.when(t == 0)
    def _():
        ubuf_ref[:, :HIST_ROWS, :] = jnp.zeros((1, HIST_ROWS, POOL_W), F32)
        s_ref[...] = jnp.zeros(s_ref.shape, F32)

    for c in range(T // C):
        rows = slice(c * C, (c + 1) * C)
        hn = _rmsnorm(h_ref[rows, :], gmix_ref[...]).astype(BF16)

        def proj(j, hn=hn):
            return _dot(hn, win_ref[:, j * POOL_W:(j + 1) * POOL_W])

        ubuf_ref[0, HIST_ROWS + c * C:HIST_ROWS + (c + 1) * C, :] = proj(0)
        gp = proj(1)
        pos = offset + t * T + c * C + lax.broadcasted_iota(jnp.int32, (C, POOL_GW), 0)
        pooled = _pool_inputs(ubuf_ref.at[:, c * C:HIST_ROWS + (c + 1) * C, :], pos, C)
        q, k, v, gr = proj(2), proj(3), proj(4).astype(BF16), proj(5)
        _pool_outputs(pooled, gp, wpool_ref, pscale_ref, mix_ref.at[rows, :])

        cos, sin = cos_ref[rows, :], sin_ref[rows, :]
        head_cols = [slice(hd * RET_DK, (hd + 1) * RET_DK) for hd in range(RET_HEADS)]
        scores, cross = [], []
        for hd, cols in enumerate(head_cols):
            qr = _rotate(q[:, cols], cos, sin).astype(BF16)
            kr = _rotate(k[:, cols], cos, sin) * (RET_DK ** -0.5)
            kd = (kr * _both_halves(kdec_ref, hd)).astype(BF16)
            scores.append(_dot_nt(qr, kr.astype(BF16)))
            s_old = s_ref[hd]
            cross.append(_dot(qr, s_old.astype(BF16)) * _both_halves(qdec_ref, hd))
            s_ref[hd] = s_old * cdec_ref[hd] + _dot_tn(kd, v[:, cols])
        for hd, cols in enumerate(head_cols):
            p = (scores[hd] * dmat_ref[hd]).astype(BF16)
            o = _dot(p, v[:, cols]) + cross[hd]
            on = _group_norm_gate(o, ggn_ref[:, cols], _silu(gr[:, cols]))
            mix_ref[rows, POOL_W + hd * RET_DV:POOL_W + (hd + 1) * RET_DV] = on.astype(mix_ref.dtype)

    @pl.when(t == pl.num_programs(1) - 1)
    def _():
        hist_ref[...] = ubuf_ref[0, T + 1:T + HIST_ROWS, :]

    ubuf_ref[0, :HIST_ROWS, :] = ubuf_ref[0, T:T + HIST_ROWS, :]

    for side in sides:
        side.tail()


def _mixer_sample_kernel(h_ref, cos_ref, sin_ref, dmat_ref, qdec_ref, kdec_ref, cdec_ref,
                         hist0_ref, s0_ref, gmix_ref, win_ref, wpool_ref, pscale_ref, ggn_ref,
                         states_hbm,
                         mix_ref, hist_ref, s_ref,
                         ubuf_ref, qr_ref, kdt_ref, v_ref, oin_ref, ox_ref, gate_ref,
                         *, offset, seq_rows):
    del states_hbm
    rows = h_ref.shape[0]
    s = pl.program_id(0)

    @pl.when(s == 0)
    def _():
        n_seq = rows // seq_rows
        hn = _rmsnorm(h_ref[...], gmix_ref[...]).astype(BF16)

        def proj(j):
            return _dot(hn, win_ref[:, j * POOL_W:(j + 1) * POOL_W])

        ubuf_ref[:, :HIST_ROWS, :] = hist0_ref[...]
        ubuf_ref[:, HIST_ROWS:, :] = proj(0).reshape(n_seq, seq_rows, POOL_W)
        gp = proj(1)
        row = lax.broadcasted_iota(jnp.int32, (rows, POOL_GW), 0)
        pos = offset + lax.rem(row, seq_rows)
        pooled = _pool_inputs(ubuf_ref, pos, seq_rows)
        hist_ref[...] = ubuf_ref[:, seq_rows + 1:seq_rows + HIST_ROWS, :]
        q, k = proj(2), proj(3)
        _pool_outputs(pooled, gp, wpool_ref, pscale_ref, mix_ref)

        v_ref[...] = proj(4).astype(BF16)
        gate_ref[...] = _silu(proj(5))
        cos, sin = cos_ref[...], sin_ref[...]
        for hd in range(RET_HEADS):
            cols = slice(hd * RET_DK, (hd + 1) * RET_DK)
            qr = _rotate(q[:, cols], cos, sin).astype(BF16)
            kr = _rotate(k[:, cols], cos, sin) * (RET_DK ** -0.5)
            qr_ref[:, cols] = qr
            kdt_ref[hd] = (kr * _both_halves(kdec_ref, hd)).T.astype(BF16)
            p = (_dot_nt(qr, kr.astype(BF16)) * dmat_ref[hd]).astype(BF16)
            oin_ref[:, cols] = _dot(p, v_ref[:, cols])

    row_id = lax.broadcasted_iota(jnp.int32, (rows, RET_DV), 0)
    for i in range(s0_ref.shape[0]):
        r0 = pl.multiple_of((s * s0_ref.shape[0] + i) * seq_rows, seq_rows)
        in_seq = (row_id >= r0) & (row_id < r0 + seq_rows)
        for hd in range(RET_HEADS):
            cols = slice(hd * RET_DK, (hd + 1) * RET_DK)
            s_old = s0_ref[i, hd]
            ox_ref[pl.ds(r0, seq_rows), cols] = _dot(qr_ref[pl.ds(r0, seq_rows), cols],
                                                     s_old.astype(BF16))
            vh = v_ref[:, cols]
            v_seq = jnp.where(in_seq, vh, jnp.zeros_like(vh))
            s_ref[i, hd] = s_old * cdec_ref[hd] + _dot(kdt_ref[hd], v_seq)

    @pl.when(s == pl.num_programs(0) - 1)
    def _():
        for hd in range(RET_HEADS):
            cols = slice(hd * RET_DK, (hd + 1) * RET_DK)
            o = oin_ref[:, cols] + ox_ref[:, cols] * _both_halves(qdec_ref, hd)
            on = _group_norm_gate(o, ggn_ref[:, cols], gate_ref[:, cols])
            mix_ref[:, POOL_W + hd * RET_DV:POOL_W + (hd + 1) * RET_DV] = on.astype(mix_ref.dtype)


def _output_body(h_ref, mix_ref, p_ref, wout_ref, gple_ref, wpg_ref, wple_ref, gfin_ref,
                 out_ref, final_norm):
    h = h_ref[...] + _dot(mix_ref[...], wout_ref[...])
    hn = _rmsnorm(h, gple_ref[...]).astype(BF16)
    gate = jax.nn.sigmoid(_dot(hn, wpg_ref[...]))
    h = h + gate * _dot(p_ref[...].astype(BF16), wple_ref[...].astype(BF16))
    if final_norm:
        h = _rmsnorm(h, gfin_ref[...])
    out_ref[...] = h


def _output_kernel(h_ref, mix_ref, p_ref, hs_ref, mixs_ref, ps_ref,
                   wout_ref, gple_ref, wpg_ref, wple_ref, gfin_ref, *rest,
                   final_norm, next_layer, n_prompt_steps):
    i = pl.program_id(0)
    side = None
    if next_layer is None:
        out_ref, outs_ref = rest
    else:
        win_next_hbm, out_ref, outs_ref, win_next_bf16_hbm, win_in, win_out, side_sems = rest
        side = _SideConvert(win_next_hbm, next_layer, win_next_bf16_hbm, win_in, win_out,
                            side_sems, i, n_prompt_steps)
        side.head()

    @pl.when(i == n_prompt_steps)
    def _():
        _output_body(hs_ref, mixs_ref, ps_ref, wout_ref, gple_ref, wpg_ref, wple_ref, gfin_ref,
                     outs_ref, final_norm)

    @pl.when(i != n_prompt_steps)
    def _():
        _output_body(h_ref, mix_ref, p_ref, wout_ref, gple_ref, wpg_ref, wple_ref, gfin_ref,
                     out_ref, final_norm)

    if side is not None:
        side.tail()


def _retention_tables(rows, seq_rows):
    lg = np.log(1.0 - 2.0 ** (-5.0 - np.arange(RET_HEADS, dtype=np.float64)))
    n = np.arange(rows)
    r = (n % seq_rows).astype(np.float64)
    same = (n // seq_rows)[:, None] == (n // seq_rows)[None, :]
    diff = r[:, None] - r[None, :]
    keep = same & (diff >= 0)
    dmat = np.where(keep[None], np.exp(np.maximum(diff, 0.0)[None] * lg[:, None, None]), 0.0)
    qdec = np.exp((r[None, :] + 1.0) * lg[:, None])
    kdec = np.exp((seq_rows - 1.0 - r[None, :]) * lg[:, None])
    cdec = np.exp(seq_rows * lg)
    bcast = lambda a: np.broadcast_to(a[:, :, None], (RET_HEADS, rows, HALF))
    return tuple(jnp.asarray(a, F32) for a in (dmat, bcast(qdec), bcast(kdec), cdec))


def _rope_tables(pos):
    inv = 1.0 / (ROPE_BASE ** (np.arange(HALF, dtype=np.float64) / HALF))
    ang = pos.astype(np.float64)[:, None] * inv[None, :]
    return jnp.asarray(np.cos(ang), F32), jnp.asarray(np.sin(ang), F32)


def _mixer_prompt(h, layer, offset, tables, rope, w, win_bf16, states, fill_shape=None):
    B, L, _ = h.shape
    T = ROW_TILE
    n_steps = B * (L // T)
    dmat, qdec, kdec, cdec = tables
    cos, sin = rope
    const = lambda *idx: (lambda b, t: idx)
    in_specs = [
        pl.BlockSpec((None, T, D_MODEL), lambda b, t: (b, t, 0)),
        pl.BlockSpec((T, HALF), lambda b, t: (t, 0)),
        pl.BlockSpec((T, HALF), lambda b, t: (t, 0)),
        _resident((RET_HEADS, RET_CHUNK, RET_CHUNK), const(0, 0, 0)),
        _resident((RET_HEADS, RET_CHUNK, HALF), const(0, 0, 0)),
        _resident((RET_HEADS, RET_CHUNK, HALF), const(0, 0, 0)),
        pl.BlockSpec(memory_space=pltpu.SMEM),
        _resident((None, 1, D_MODEL), const(layer, 0, 0)),
        _resident((D_MODEL, IN_W), const(0, 0)),
        _resident((None, len(POOL_WINDOWS), POOL_GW, POOL_GW), const(layer, 0, 0, 0)),
        _resident((None, 1, POOL_W), const(layer, 0, 0)),
        _resident((None, 1, RET_W), const(layer, 0, 0)),
        pl.BlockSpec(memory_space=pl.ANY),
        pl.BlockSpec(memory_space=pl.ANY),
        pl.BlockSpec(memory_space=pl.ANY),
    ]
    out_specs = [
        pl.BlockSpec((None, T, MIX_W), lambda b, t: (b, t, 0)),
        pl.BlockSpec((None, POOL_HIST, POOL_W), lambda b, t: (b, 0, 0)),
        pl.BlockSpec((None, None, RET_HEADS, RET_DK, RET_DV), lambda b, t: (layer, b, 0, 0, 0)),
        pl.BlockSpec(memory_space=pl.ANY),
        pl.BlockSpec(memory_space=pl.ANY),
    ]
    out_shape = [
        jax.ShapeDtypeStruct((B, L, MIX_W), BF16),
        jax.ShapeDtypeStruct((B, POOL_HIST, POOL_W), F32),
        jax.ShapeDtypeStruct(states.shape, F32),
        jax.ShapeDtypeStruct((MIX_W, D_MODEL), BF16),
        jax.ShapeDtypeStruct((D_MODEL, D_MODEL), BF16),
    ]
    scratch_shapes = [
        pltpu.VMEM((1, HIST_ROWS + T, POOL_W), F32),
        pltpu.VMEM((MIX_W // n_steps, D_MODEL), F32),
        pltpu.VMEM((MIX_W // n_steps, D_MODEL), BF16),
        pltpu.SemaphoreType.DMA((2,)),
        pltpu.VMEM((D_MODEL // n_steps, D_MODEL), F32),
        pltpu.VMEM((D_MODEL // n_steps, D_MODEL), BF16),
        pltpu.SemaphoreType.DMA((2,)),
    ]
    if fill_shape is not None:
        out_specs.append(pl.BlockSpec(memory_space=pl.ANY))
        out_shape.append(jax.ShapeDtypeStruct(fill_shape, F32))
        slab = fill_shape[0] * fill_shape[1] // n_steps
        scratch_shapes += [pltpu.VMEM((slab,) + tuple(fill_shape[2:]), F32),
                           pltpu.SemaphoreType.DMA((1,))]
    return pl.pallas_call(
        functools.partial(_mixer_prompt_kernel, offset=offset, layer=layer,
                          fill=fill_shape is not None),
        grid=(B, L // T),
        in_specs=in_specs,
        out_specs=out_specs,
        out_shape=out_shape,
        scratch_shapes=scratch_shapes,
        compiler_params=pltpu.CompilerParams(
            dimension_semantics=("arbitrary", "arbitrary"),
            vmem_limit_bytes=VMEM_LIMIT_BYTES),
        input_output_aliases={len(in_specs) - 1: 2},
        name=f"mixer_prompt_l{layer}",
    )(h, cos, sin, dmat, qdec, kdec, cdec, w["g_mix"], win_bf16, w["w_pool"],
      w["pool_scale"], w["g_gn"], w["w_out"], w["w_pg"], states)


def _mixer_sample(h2d, hist0, s0, layer, offset, seq_rows, tables, rope, w, win_bf16, states):
    rows = h2d.shape[0]
    n_seq = rows // seq_rows
    dmat, qdec, kdec, cdec = tables
    cos, sin = rope
    const = lambda *idx: (lambda s: idx)
    in_specs = [
        _resident((rows, D_MODEL), const(0, 0)),
        _resident((rows, HALF), const(0, 0)),
        _resident((rows, HALF), const(0, 0)),
        _resident((RET_HEADS, rows, rows), const(0, 0, 0)),
        _resident((RET_HEADS, rows, HALF), const(0, 0, 0)),
        _resident((RET_HEADS, rows, HALF), const(0, 0, 0)),
        pl.BlockSpec(memory_space=pltpu.SMEM),
        _resident((None, n_seq, HIST_ROWS, POOL_W), const(layer, 0, 0, 0)),
        pl.BlockSpec((None, SEQS_PER_STEP, RET_HEADS, RET_DK, RET_DV),
                     lambda s: (layer, s, 0, 0, 0)),
        _resident((None, 1, D_MODEL), const(layer, 0, 0)),
        _resident((D_MODEL, IN_W), const(0, 0)),
        _resident((None, len(POOL_WINDOWS), POOL_GW, POOL_GW), const(layer, 0, 0, 0)),
        _resident((None, 1, POOL_W), const(layer, 0, 0)),
        _resident((None, 1, RET_W), const(layer, 0, 0)),
        pl.BlockSpec(memory_space=pl.ANY),
    ]
    out_specs = [
        pl.BlockSpec((rows, MIX_W), const(0, 0)),
        pl.BlockSpec((n_seq, POOL_HIST, POOL_W), const(0, 0, 0)),
        pl.BlockSpec((None, SEQS_PER_STEP, RET_HEADS, RET_DK, RET_DV),
                     lambda s: (layer, s, 0, 0, 0)),
    ]
    out_shape = [
        jax.ShapeDtypeStruct((rows, MIX_W), BF16),
        jax.ShapeDtypeStruct((n_seq, POOL_HIST, POOL_W), F32),
        jax.ShapeDtypeStruct(states.shape, F32),
    ]
    scratch_shapes = [
        pltpu.VMEM((n_seq, HIST_ROWS + seq_rows, POOL_W), F32),
        pltpu.VMEM((rows, RET_W), BF16),
        pltpu.VMEM((RET_HEADS, RET_DK, rows), BF16),
        pltpu.VMEM((rows, RET_W), BF16),
        pltpu.VMEM((rows, RET_W), F32),
        pltpu.VMEM((rows, RET_W), F32),
        pltpu.VMEM((rows, RET_W), F32),
    ]
    return pl.pallas_call(
        functools.partial(_mixer_sample_kernel, offset=offset, seq_rows=seq_rows),
        grid=(n_seq // SEQS_PER_STEP,),
        in_specs=in_specs,
        out_specs=out_specs,
        out_shape=out_shape,
        scratch_shapes=scratch_shapes,
        compiler_params=pltpu.CompilerParams(
            dimension_semantics=("arbitrary",),
            vmem_limit_bytes=VMEM_LIMIT_BYTES),
        input_output_aliases={len(in_specs) - 1: 2},
        name=f"mixer_sample_l{layer}",
    )(h2d, cos, sin, dmat, qdec, kdec, cdec, hist0, s0, w["g_mix"], win_bf16, w["w_pool"],
      w["pool_scale"], w["g_gn"], states)


def _output_call(h_p, mix_p, p2d_p, h_s, mix_s, p2d_s, layer, w, wout_bf16, wpg_bf16):
    rows_p, rows_s = h_p.shape[0], h_s.shape[0]
    T = min(OUT_ROW_TILE, rows_p)
    n = rows_p // T
    final_norm = layer == DEPTH - 1
    next_layer = None if final_norm else layer + 1
    const = lambda *idx: (lambda i: idx)
    tile = lambda i: jnp.minimum(i, n - 1)
    in_specs = [
        pl.BlockSpec((T, D_MODEL), lambda i: (tile(i), 0)),
        pl.BlockSpec((T, MIX_W), lambda i: (tile(i), 0)),
        pl.BlockSpec((None, T, PLE_DIM), lambda i: (layer, tile(i), 0)),
        _resident((rows_s, D_MODEL), const(0, 0)),
        _resident((rows_s, MIX_W), const(0, 0)),
        _resident((None, rows_s, PLE_DIM), const(layer, 0, 0)),
        _resident((MIX_W, D_MODEL), const(0, 0)),
        _resident((None, 1, D_MODEL), const(layer, 0, 0)),
        _resident((D_MODEL, D_MODEL), const(0, 0)),
        _resident((None, PLE_DIM, D_MODEL), const(layer, 0, 0)),
        _resident((1, D_MODEL), const(0, 0)),
    ]
    out_specs = [pl.BlockSpec((T, D_MODEL), lambda i: (tile(i), 0)),
                 pl.BlockSpec((rows_s, D_MODEL), const(0, 0))]
    out_shape = [jax.ShapeDtypeStruct((rows_p, D_MODEL), F32),
                 jax.ShapeDtypeStruct((rows_s, D_MODEL), F32)]
    operands = [h_p, mix_p, p2d_p, h_s, mix_s, p2d_s,
                wout_bf16, w["g_ple"], wpg_bf16, w["w_ple"], w["g_final"]]
    scratch_shapes = []
    if next_layer is not None:
        in_specs.append(pl.BlockSpec(memory_space=pl.ANY))
        operands.append(w["w_in"])
        out_specs.append(pl.BlockSpec(memory_space=pl.ANY))
        out_shape.append(jax.ShapeDtypeStruct((D_MODEL, IN_W), BF16))
        scratch_shapes = [
            pltpu.VMEM((D_MODEL // n, IN_W), F32),
            pltpu.VMEM((D_MODEL // n, IN_W), BF16),
            pltpu.SemaphoreType.DMA((2,)),
        ]
    return pl.pallas_call(
        functools.partial(_output_kernel, final_norm=final_norm, next_layer=next_layer,
                          n_prompt_steps=n),
        grid=(n + 1,),
        in_specs=in_specs,
        out_specs=out_specs,
        out_shape=out_shape,
        scratch_shapes=scratch_shapes,
        compiler_params=pltpu.CompilerParams(
            dimension_semantics=("arbitrary",),
            vmem_limit_bytes=VMEM_LIMIT_BYTES),
        name=f"output_l{layer}",
    )(*operands)


def _trunks(x_p, p_p, x_s, p_s, hist0_s, s0_s, offset_s, w):
    B, L, _ = x_p.shape
    Bs, Ls, _ = x_s.shape
    rows_p, rows_s = B * L, Bs * Ls
    tables_p = _retention_tables(RET_CHUNK, RET_CHUNK)
    tables_s = _retention_tables(rows_s, Ls)
    rope_p = _rope_tables(np.arange(L))
    rope_s = _rope_tables(offset_s + np.arange(rows_s) % Ls)
    p2d_p = p_p.reshape(DEPTH, rows_p, PLE_DIM)
    p2d_s = p_s.reshape(DEPTH, rows_s, PLE_DIM)
    hist0_s = jnp.pad(hist0_s, ((0, 0), (0, 0), (HIST_ROWS - POOL_HIST, 0), (0, 0)))
    h_p, h_s = x_p, x_s.reshape(rows_s, D_MODEL)
    hists_p, hists_s = [], []
    states_p = jnp.zeros((DEPTH, B, RET_HEADS, RET_DK, RET_DV), F32)
    states_s_shape = (DEPTH, Bs, RET_HEADS, RET_DK, RET_DV)
    n_steps = B * (L // ROW_TILE)
    slab = DEPTH * Bs // n_steps
    fill = slab > 0 and slab * n_steps == DEPTH * Bs and Bs % slab == 0
    states_s = None if fill else jnp.zeros(states_s_shape, F32)
    win_b = w["w_in"][0].astype(BF16)
    for layer in range(DEPTH):
        outs = _mixer_prompt(h_p, layer, 0, tables_p, rope_p, w, win_b, states_p,
                             fill_shape=states_s_shape if fill and layer == 0 else None)
        mix_p, hist, states_p, wout_b, wpg_b = outs[:5]
        if fill and layer == 0:
            states_s = outs[5]
        hists_p.append(hist)
        mix_s, hist, states_s = _mixer_sample(h_s, hist0_s, s0_s, layer, offset_s, Ls, tables_s,
                                              rope_s, w, win_b, states_s)
        hists_s.append(hist)
        outs = _output_call(h_p.reshape(rows_p, D_MODEL), mix_p.reshape(rows_p, MIX_W), p2d_p,
                            h_s, mix_s, p2d_s, layer, w, wout_b, wpg_b)
        h_p, h_s = outs[0].reshape(B, L, D_MODEL), outs[1]
        if layer + 1 < DEPTH:
            win_b = outs[2]
    stack = lambda xs: jnp.stack(xs, axis=0)
    return (h_p, h_s.reshape(Bs, Ls, D_MODEL), stack(hists_p), states_p, stack(hists_s), states_s)


def kernel(x_prompt, x_sample, p_prompt, p_sample, state_pool, state_ret, g_mix, w_in, w_pool,
           pool_scale, g_gn, w_out, g_ple, w_pg, w_ple, g_final):
    past_len = 1024
    row = lambda a: a.reshape(a.shape[0], 1, a.shape[1])
    w = {
        "g_mix": row(g_mix), "w_in": w_in, "w_pool": w_pool,
        "pool_scale": row(pool_scale), "g_gn": row(g_gn), "w_out": w_out,
        "g_ple": row(g_ple), "w_pg": w_pg, "w_ple": w_ple,
        "g_final": g_final.reshape(1, D_MODEL),
    }
    return _trunks(x_prompt, p_prompt, x_sample, p_sample, state_pool, state_ret, past_len, w)
```
